```python
import math
import jax, jax.numpy as jnp
from jax import lax
import numpy as np

D_MODEL = 1024
BATCH = 4
SEQ = 8192
DEPTH = 2

GRID_W = 64
MEM_LEN = 256
N_GROUPS = 4
GROUP_W = D_MODEL // N_GROUPS
FNET_GROUPS = 4
FNET_CH = GROUP_W // FNET_GROUPS
CONV_K = 31
GQA_HEAD_DIM = 64
GQA_Q_HEADS = GROUP_W // GQA_HEAD_DIM
GQA_KV_HEADS = 2
MLA_HEADS = 4
MLA_NOPE = 64
MLA_ROPE = 32
MLA_V = GROUP_W // MLA_HEADS
MLA_Q_RANK = 256
MLA_KV_RANK = 128
X_HEADS = 4
X_HEAD_DIM = D_MODEL // X_HEADS
N_EXPERTS = 32
TOP_K = 4
D_FF = D_MODEL
SWIGLU_LIMIT = 7.0
SWIGLU_ALPHA = 1.702
MOE_BLOCK = 256
Q_BLOCK = 128
ROPE_THETA = 10000.0
LN_EPS = 1e-5
RMS_EPS = 1e-6
DN_ALPHA = (2 * DEPTH) ** 0.25
DN_BETA = (8 * DEPTH) ** -0.25

F_COLS = GROUP_W
C_COLS = 2 * GROUP_W
GQ_Q_COLS = GQA_Q_HEADS * GQA_HEAD_DIM
GQ_KV_COLS = GQA_KV_HEADS * GQA_HEAD_DIM
IN_COLS = F_COLS + C_COLS + GQ_Q_COLS + 2 * GQ_KV_COLS + MLA_Q_RANK + MLA_KV_RANK + MLA_ROPE
SPLITS = (F_COLS,
          F_COLS + C_COLS,
          F_COLS + C_COLS + GQ_Q_COLS,
          F_COLS + C_COLS + GQ_Q_COLS + GQ_KV_COLS,
          F_COLS + C_COLS + GQ_Q_COLS + 2 * GQ_KV_COLS,
          F_COLS + C_COLS + GQ_Q_COLS + 2 * GQ_KV_COLS + MLA_Q_RANK,
          F_COLS + C_COLS + GQ_Q_COLS + 2 * GQ_KV_COLS + MLA_Q_RANK + MLA_KV_RANK)

kernel_name = "hybrid_parallel_group_encoder"


def layer_norm(x, g, b):
    xf = x.astype(jnp.float32)
    mu = jnp.mean(xf, axis=-1, keepdims=True)
    var = jnp.mean(jnp.square(xf - mu), axis=-1, keepdims=True)
    return ((xf - mu) * lax.rsqrt(var + LN_EPS)).astype(x.dtype) * g + b


def rms_norm(x, g):
    xf = x.astype(jnp.float32)
    r = lax.rsqrt(jnp.mean(jnp.square(xf), axis=-1, keepdims=True) + RMS_EPS)
    return (xf * r).astype(x.dtype) * g


def rope_tables(pos, dim):
    inv = ROPE_THETA ** (-jnp.arange(0, dim, 2, dtype=jnp.float32) / dim)
    ang = pos.astype(jnp.float32)[:, None] * inv[None, :]
    return (jnp.cos(ang), jnp.sin(ang))


def apply_rope(x, cos, sin):
    half = x.shape[-1] // 2
    c = cos[:, None, :].astype(x.dtype)
    s = sin[:, None, :].astype(x.dtype)
    x1, x2 = x[..., :half], x[..., half:]
    return jnp.concatenate([x1 * c - x2 * s, x2 * c + x1 * s], axis=-1)


def axial_rope(x, tabs):
    cos_r, sin_r, cos_c, sin_c = tabs
    half = x.shape[-1] // 2
    return jnp.concatenate([apply_rope(x[..., :half], cos_r, sin_r),
                            apply_rope(x[..., half:], cos_c, sin_c)], axis=-1)


def blocked_attention(q, k, v, scale):
    B, S, Hq, dk = q.shape
    Hkv = k.shape[2]
    G = Hq // Hkv
    dv = v.shape[-1]
    nb = S // Q_BLOCK
    qb = q.reshape(B, nb, Q_BLOCK, Hkv, G, dk).transpose(1, 0, 2, 3, 4, 5)

    def one_block(qblk):
        s = jnp.einsum('bqhgd,bkhd->bhgqk', qblk, k).astype(jnp.float32) * scale
        p = jax.nn.softmax(s, axis=-1).astype(v.dtype)
        return jnp.einsum('bhgqk,bkhd->bqhgd', p, v)

    o = lax.map(one_block, qb)
    return o.transpose(1, 0, 2, 3, 4, 5).reshape(B, S, Hq * dv)


def hybrid_mixer(h, rope_g, rope_m, w_in, w_f, b_f, dw_w, dw_b, cg, cb, w_pw, b_pw,
                 qg, kg, mqg, w_uq, mkg, w_ukv, grp_g, w_o):
    B, S, _ = h.shape
    z = h @ w_in
    zf, zc, zq, zk, zv, zcq, zckv, zkr = jnp.split(z, SPLITS, axis=-1)

    u = zf.reshape(B, S, FNET_GROUPS, FNET_CH).astype(jnp.float32)
    yf = jnp.fft.fft2(u, axes=(1, 3)).real.astype(h.dtype).reshape(B, S, GROUP_W) @ w_f + b_f

    a, g = jnp.split(zc, 2, axis=-1)
    u = a * jax.nn.sigmoid(g)
    u = lax.conv_general_dilated(u, dw_w[:, None, :], (1,), [(CONV_K // 2, CONV_K // 2)],
                                 dimension_numbers=('NWC', 'WIO', 'NWC'),
                                 feature_group_count=GROUP_W) + dw_b
    yc = jax.nn.silu(layer_norm(u, cg, cb)) @ w_pw + b_pw

    q = rms_norm(zq.reshape(B, S, GQA_Q_HEADS, GQA_HEAD_DIM), qg)
    k = rms_norm(zk.reshape(B, S, GQA_KV_HEADS, GQA_HEAD_DIM), kg)
    v = zv.reshape(B, S, GQA_KV_HEADS, GQA_HEAD_DIM)
    yg = blocked_attention(axial_rope(q, rope_g), axial_rope(k, rope_g), v, GQA_HEAD_DIM ** -0.5)

    qm = (rms_norm(zcq, mqg) @ w_uq).reshape(B, S, MLA_HEADS, MLA_NOPE + MLA_ROPE)
    kvm = (rms_norm(zckv, mkg) @ w_ukv).reshape(B, S, MLA_HEADS, MLA_NOPE + MLA_V)
    q_nope, q_rope = qm[..., :MLA_NOPE], qm[..., MLA_NOPE:]
    k_nope, vm = kvm[..., :MLA_NOPE], kvm[..., MLA_NOPE:]
    k_rope = axial_rope(zkr[:, :, None, :], rope_m)
    qm = jnp.concatenate([q_nope, axial_rope(q_rope, rope_m)], axis=-1)
    km = jnp.concatenate([k_nope, jnp.broadcast_to(k_rope, (B, S, MLA_HEADS, MLA_ROPE))], axis=-1)
    ym = blocked_attention(qm, km, vm, (MLA_NOPE + MLA_ROPE) ** -0.5)

    y = rms_norm(jnp.stack([yf, yc, yg, ym], axis=2), grp_g).reshape(B, S, D_MODEL)
    return y @ w_o


def memory_cross_attention(h, mem, w_xq, w_xk, w_xv, w_xo):
    B, S, _ = h.shape
    M = mem.shape[1]
    q = (h @ w_xq).reshape(B, S, X_HEADS, X_HEAD_DIM)
    k = (mem @ w_xk).reshape(B, M, X_HEADS, X_HEAD_DIM)
    v = (mem @ w_xv).reshape(B, M, X_HEADS, X_HEAD_DIM)
    return blocked_attention(q, k, v, X_HEAD_DIM ** -0.5) @ w_xo


def routed_experts(h, w_router, b_router, w_gu, b_gu, w_down, b_down):
    B, S, D = h.shape
    T = B * S
    xt = h.reshape(T, D)
    logits = (xt @ w_router + b_router).astype(jnp.float32)
    top_v, top_i = lax.top_k(logits, TOP_K)
    gates = jax.nn.softmax(top_v, axis=-1).astype(h.dtype)

    N = T * TOP_K
    flat_e = top_i.reshape(N)
    order = jnp.argsort(flat_e)
    sorted_e = flat_e[order]
    counts = jnp.bincount(flat_e, length=N_EXPERTS)
    padded = ((counts + MOE_BLOCK - 1) // MOE_BLOCK) * MOE_BLOCK
    start = jnp.cumsum(counts) - counts
    pend = jnp.cumsum(padded)
    pstart = pend - padded
    dest = pstart[sorted_e] + (jnp.arange(N) - start[sorted_e])
    n_blocks = -(-N // MOE_BLOCK) + N_EXPERTS
    P = n_blocks * MOE_BLOCK
    slot_tok = jnp.zeros((P,), jnp.int32).at[dest].set((order // TOP_K).astype(jnp.int32))
    slot_w = jnp.zeros((P,), h.dtype).at[dest].set(gates.reshape(N)[order])
    block_e = jnp.minimum(jnp.searchsorted(pend, jnp.arange(n_blocks) * MOE_BLOCK, side='right'),
                          N_EXPERTS - 1)
    xs = xt[slot_tok].reshape(n_blocks, MOE_BLOCK, D)

    def expert_block(args):
        xb, e = args
        gu = xb @ w_gu[e] + b_gu[e]
        gate, up = gu[:, :D_FF], gu[:, D_FF:]
        gate = jnp.minimum(gate, SWIGLU_LIMIT)
        up = jnp.clip(up, -SWIGLU_LIMIT, SWIGLU_LIMIT)
        act = (up + 1.0) * (gate * jax.nn.sigmoid(SWIGLU_ALPHA * gate))
        return act @ w_down[e] + b_down[e]

    ys = lax.map(expert_block, (xs, block_e)).reshape(P, D)
    out = jnp.zeros((T, D), h.dtype).at[slot_tok].add(ys * slot_w[:, None])
    return out.reshape(B, S, D)


def setup_inputs(seed: int = 0) -> dict:
    key = jax.random.key(seed)
    ks = iter(jax.random.split(key, 48))
    L, D, E, F, G = DEPTH, D_MODEL, N_EXPERTS, D_FF, GROUP_W

    def nrm(shape, scale):
        return jax.random.normal(next(ks), shape, jnp.float32) * scale

    def gain(shape):
        return 1.0 + nrm(shape, 0.02)

    return {
        "x": nrm((BATCH, SEQ, D), 1.0),
        "mem": nrm((BATCH, MEM_LEN, D), 1.0),
        "ln_in_g": gain((D,)),
        "ln_in_b": nrm((D,), 0.02),
        "w_in": nrm((L, D, IN_COLS), D ** -0.5),
        "w_f": nrm((L, G, G), G ** -0.5),
        "b_f": nrm((L, G), 0.02),
        "dw_w": nrm((L, CONV_K, G), CONV_K ** -0.5),
        "dw_b": nrm((L, G), 0.02),
        "conv_ln_g": gain((L, G)),
        "conv_ln_b": nrm((L, G), 0.02),
        "w_pw": nrm((L, G, G), G ** -0.5),
        "b_pw": nrm((L, G), 0.02),
        "q_norm_g": gain((L, GQA_HEAD_DIM)),
        "k_norm_g": gain((L, GQA_HEAD_DIM)),
        "mla_q_norm_g": gain((L, MLA_Q_RANK)),
        "w_uq": nrm((L, MLA_Q_RANK, MLA_HEADS * (MLA_NOPE + MLA_ROPE)), MLA_Q_RANK ** -0.5),
        "mla_kv_norm_g": gain((L, MLA_KV_RANK)),
        "w_ukv": nrm((L, MLA_KV_RANK, MLA_HEADS * (MLA_NOPE + MLA_V)), MLA_KV_RANK ** -0.5),
        "grp_norm_g": gain((L, N_GROUPS, G)),
        "w_o": nrm((L, D, D), D ** -0.5 * DN_BETA),
        "ln1_g": gain((L, D)),
        "ln1_b": nrm((L, D), 0.02),
        "w_xq": nrm((L, D, D), D ** -0.5),
        "w_xk": nrm((L, D, D), D ** -0.5),
        "w_xv": nrm((L, D, D), D ** -0.5),
        "w_xo": nrm((L, D, D), D ** -0.5 * DN_BETA),
        "ln2_g": gain((L, D)),
        "ln2_b": nrm((L, D), 0.02),
        "w_router": nrm((L, D, E), D ** -0.5),
        "b_router": nrm((L, E), 0.01),
        "w_gu": nrm((L, E, D, 2 * F), D ** -0.5),
        "b_gu": nrm((L, E, 2 * F), 0.02),
        "w_down": nrm((L, E, F, D), F ** -0.5 * DN_BETA),
        "b_down": nrm((L, E, D), 0.02),
        "ln3_g": gain((L, D)),
        "ln3_b": nrm((L, D), 0.02),
    }


def reference(x, mem, ln_in_g, ln_in_b, w_in, w_f, b_f, dw_w, dw_b, conv_ln_g, conv_ln_b,
              w_pw, b_pw, q_norm_g, k_norm_g, mla_q_norm_g, w_uq, mla_kv_norm_g, w_ukv,
              grp_norm_g, w_o, ln1_g, ln1_b, w_xq, w_xk, w_xv, w_xo, ln2_g, ln2_b,
              w_router, b_router, w_gu, b_gu, w_down, b_down, ln3_g, ln3_b):
    S = x.shape[1]
    ROWS = S // GRID_W
    row = jnp.repeat(jnp.arange(ROWS, dtype=jnp.int32), GRID_W)
    col = jnp.tile(jnp.arange(GRID_W, dtype=jnp.int32), ROWS)
    rope_g = rope_tables(row, GQA_HEAD_DIM // 2) + rope_tables(col, GQA_HEAD_DIM // 2)
    rope_m = rope_tables(row, MLA_ROPE // 2) + rope_tables(col, MLA_ROPE // 2)

    h = layer_norm(x, ln_in_g, ln_in_b)
    for l in range(DEPTH):
        mix = hybrid_mixer(h, rope_g, rope_m, w_in[l], w_f[l], b_f[l], dw_w[l], dw_b[l],
                           conv_ln_g[l], conv_ln_b[l], w_pw[l], b_pw[l], q_norm_g[l], k_norm_g[l],
                           mla_q_norm_g[l], w_uq[l], mla_kv_norm_g[l], w_ukv[l], grp_norm_g[l], w_o[l])
        h = layer_norm(DN_ALPHA * h + mix, ln1_g[l], ln1_b[l])
        xa = memory_cross_attention(h, mem, w_xq[l], w_xk[l], w_xv[l], w_xo[l])
        h = layer_norm(DN_ALPHA * h + xa, ln2_g[l], ln2_b[l])
        ff = routed_experts(h, w_router[l], b_router[l], w_gu[l], b_gu[l], w_down[l], b_down[l])
        h = layer_norm(DN_ALPHA * h + ff, ln3_g[l], ln3_b[l])
    return h
```

```python
import functools
import math

import numpy as np
import jax
import jax.numpy as jnp
from jax import lax
from jax.experimental import pallas as pl
from jax.experimental.pallas import tpu as pltpu

F32 = jnp.float32
BF16 = jnp.bfloat16

D_MODEL = 1024
DEPTH = 2
GRID_W = 64
N_GROUPS = 4
GROUP_W = D_MODEL // N_GROUPS
FNET_CH = 64
CONV_K = 31
CONV_HALO = 16
GQA_HEAD_DIM = 64
GQA_Q_HEADS = 4
GQA_KV_HEADS = 2
MLA_HEADS = 4
MLA_NOPE = 64
MLA_ROPE = 32
MLA_V = 64
MLA_Q_RANK = 256
MLA_KV_RANK = 128
X_HEADS = 4
X_HEAD_DIM = D_MODEL // X_HEADS
N_EXPERTS = 32
TOP_K = 4
D_FF = D_MODEL
SWIGLU_LIMIT = 7.0
SWIGLU_ALPHA = 1.702
ROPE_THETA = 10000.0
LN_EPS = 1e-5
RMS_EPS = 1e-6
DN_ALPHA = (2 * DEPTH) ** 0.25

LANE = 128
HEAD_PAD = 128
IN_COLS_PAD = 1792
DFT_N1 = 64
MOE_BLOCK = 256
NEG_BIG = -1e30
VMEM_LIMIT = 56 * 1024 * 1024


def _cparams(n_axes, vmem=VMEM_LIMIT):
    return pltpu.CompilerParams(dimension_semantics=("arbitrary",) * n_axes, vmem_limit_bytes=vmem)


def _full(shape):
    nd = len(shape)
    return pl.BlockSpec(shape, lambda *a: (0,) * nd)


def _dot(a, b):
    return jnp.dot(a, b, preferred_element_type=F32)


def _split(x):
    hi = x.astype(BF16)
    lo = (x - hi.astype(F32)).astype(BF16)
    return hi, lo


def _dot3(a_hi, a_lo, b_hi, b_lo):
    return _dot(a_hi, b_hi) + (_dot(a_hi, b_lo) + _dot(a_lo, b_hi))


def _layer_norm(x, g, b):
    mu = jnp.mean(x, axis=-1, keepdims=True)
    xc = x - mu
    var = jnp.mean(xc * xc, axis=-1, keepdims=True)
    return xc * lax.rsqrt(var + LN_EPS) * g + b


def _rms(x, g):
    return x * lax.rsqrt(jnp.mean(x * x, axis=-1, keepdims=True) + RMS_EPS) * g


def _rope(x, cos, sin_signed, half):
    lane = lax.broadcasted_iota(jnp.int32, x.shape, 1)
    first = (lane % (2 * half)) < half
    partner = jnp.where(first, pltpu.roll(x, LANE - half, 1), pltpu.roll(x, half, 1))
    return x * cos + partner * sin_signed


def _ln_kernel(x_ref, g_ref, b_ref, o_ref):
    o_ref[...] = _layer_norm(x_ref[...], g_ref[...], b_ref[...])


def _ln_call(x, g, b, tm=1024):
    T, D = x.shape
    return pl.pallas_call(
        _ln_kernel,
        grid=(T // tm,),
        in_specs=[pl.BlockSpec((tm, D), lambda i: (i, 0)), _full((1, D)), _full((1, D))],
        out_specs=pl.BlockSpec((tm, D), lambda i: (i, 0)),
        out_shape=jax.ShapeDtypeStruct((T, D), F32),
        compiler_params=_cparams(1),
        name="ln_in",
    )(x, g.reshape(1, D), b.reshape(1, D))


def _inproj_kernel(h_ref, w_ref, cg_ref, sg_ref, cm_ref, sm_ref, gsum_ref, qg_ref, kg_ref,
                   mqg_ref, mkg_ref, wuq_ref, wukn_ref, wuv_ref,
                   zf_ref, u_ref, qgo_ref, kgo_ref, vgo_ref, qmo_ref, kmo_ref, vmo_ref):
    z = _dot(h_ref[...].astype(BF16), w_ref[...])
    zf_ref[...] = z[:, 0:256]
    a = z[:, 256:512]
    g = z[:, 512:768]
    u_ref[...] = a * (1.0 / (1.0 + jnp.exp(-g)))

    cg = cg_ref[...]
    sg = sg_ref[...]
    cm = cm_ref[...]
    sm = sm_ref[...]
    lane = lax.broadcasted_iota(jnp.int32, cg.shape, 1)
    low = lane < GQA_HEAD_DIM
    gsum = gsum_ref[...]

    def head_norm(x, gain):
        hi, lo = _split(x * x)
        ssq = _dot(hi, gsum) + _dot(lo, gsum)
        return x * lax.rsqrt(ssq * (1.0 / GQA_HEAD_DIM) + RMS_EPS) * gain

    scale_g = GQA_HEAD_DIM ** -0.5
    q_tiles = []
    for t in range(2):
        x = z[:, 768 + t * LANE: 768 + (t + 1) * LANE]
        x = _rope(head_norm(x, qg_ref[...]), cg, sg, GQA_HEAD_DIM // 4) * scale_g
        xr = pltpu.roll(x, GQA_HEAD_DIM, 1)
        if t == 0:
            q_tiles += [jnp.where(low, x, 0.0), jnp.where(low, xr, 0.0)]
        else:
            q_tiles += [jnp.where(low, 0.0, xr), jnp.where(low, 0.0, x)]
    qgo_ref[...] = jnp.concatenate(q_tiles, axis=1).astype(BF16)

    k = z[:, 1024:1152]
    kgo_ref[...] = _rope(head_norm(k, kg_ref[...]), cg, sg, GQA_HEAD_DIM // 4).astype(BF16)

    v = z[:, 1152:1280]
    ones_col = jnp.where(lane == GQA_HEAD_DIM, 1.0, 0.0)
    v0 = jnp.where(low, v, ones_col)
    v1 = jnp.where(low, pltpu.roll(v, GQA_HEAD_DIM, 1), ones_col)
    vgo_ref[...] = jnp.concatenate([v0, v1], axis=1).astype(BF16)

    scale_m = (MLA_NOPE + MLA_ROPE) ** -0.5
    cq = _rms(z[:, 1280:1536], mqg_ref[...]).astype(BF16)
    qm = _dot(cq, wuq_ref[...])
    qm_tiles = [_rope(qm[:, t * LANE:(t + 1) * LANE], cm, sm, MLA_ROPE // 4) * scale_m
                for t in range(MLA_HEADS)]
    qmo_ref[...] = jnp.concatenate(qm_tiles, axis=1).astype(BF16)

    ckv = _rms(z[:, 1536:1664], mkg_ref[...]).astype(BF16)
    kr = _rope(z[:, 1664:1792], cm, sm, MLA_ROPE // 4)
    kn = _dot(ckv, wukn_ref[...])
    kmo_ref[...] = (kn + jnp.concatenate([kr] * MLA_HEADS, axis=1)).astype(BF16)
    vm = _dot(ckv, wuv_ref[...])
    vmo_ref[...] = (vm + jnp.concatenate([ones_col] * MLA_HEADS, axis=1)).astype(BF16)


def _inproj_call(h, S, w_in_p, tabs, gsum, qg, kg, mqg, mkg, wuq, wukn, wuv, tm=512):
    T, D = h.shape
    nsb = S // tm
    cg, sg, cm, sm = tabs
    row = lambda w: pl.BlockSpec((tm, w), lambda i: (i, 0))
    tab = pl.BlockSpec((tm, LANE), lambda i: (i % nsb, 0))
    outs = [(256, F32), (256, F32), (512, BF16), (128, BF16), (256, BF16), (512, BF16), (512, BF16), (512, BF16)]
    return pl.pallas_call(
        _inproj_kernel,
        grid=(T // tm,),
        in_specs=[row(D), _full(w_in_p.shape), tab, tab, tab, tab, _full(gsum.shape), _full(qg.shape),
                  _full(kg.shape), _full(mqg.shape), _full(mkg.shape), _full(wuq.shape), _full(wukn.shape),
                  _full(wuv.shape)],
        out_specs=[row(w) for w, _ in outs],
        out_shape=[jax.ShapeDtypeStruct((T, w), dt) for w, dt in outs],
        compiler_params=_cparams(1),
        name="in_proj",
    )(h, w_in_p, cg, sg, cm, sm, gsum, qg, kg, mqg, mkg, wuq, wukn, wuv)


def _fft1_kernel(x_ref, chi_ref, clo_ref, o_ref):
    chi = chi_ref[...]
    clo = clo_ref[...]
    for j in range(8):
        xh, xl = _split(x_ref[0, :, j, :])
        o_ref[0, :, j, :] = _dot3(chi, clo, xh, xl)


def _fft2_kernel(re_ref, im_ref, mhi_ref, mlo_ref, chh_ref, chl_ref, wf_ref, bf_ref, o_ref):
    n2 = re_ref.shape[2]
    for j in range(8):
        ab = jnp.concatenate([re_ref[0, j], im_ref[0, j]], axis=0)
        abh, abl = _split(ab)
        ri = _dot3(mhi_ref[j], mlo_ref[j], abh, abl)
        rc = jnp.concatenate([ri[:n2], ri[n2:]], axis=1)
        rch, rcl = _split(rc)
        y = _dot3(rch, rcl, chh_ref[...], chl_ref[...])
        o_ref[0, :, j, :] = _dot(y.astype(BF16), wf_ref[...]) + bf_ref[...]


def _fnet_call(zf, B, S, consts, w_f, b_f):
    C = GROUP_W
    n1, n2 = DFT_N1, S // DFT_N1
    c1h, c1l, mh, ml, chh, chl = consts
    x4 = zf.reshape(B, n1, n2, C)
    ab = pl.pallas_call(
        _fft1_kernel,
        grid=(B, n2 // 8),
        in_specs=[pl.BlockSpec((1, n1, 8, C), lambda b, i: (b, 0, i, 0)), _full(c1h.shape), _full(c1l.shape)],
        out_specs=pl.BlockSpec((1, 2 * n1, 8, C), lambda b, i: (b, 0, i, 0)),
        out_shape=jax.ShapeDtypeStruct((B, 2 * n1, n2, C), F32),
        compiler_params=_cparams(2),
        name="fnet_stage1",
    )(x4, c1h, c1l)
    g = n1 // 8
    y = pl.pallas_call(
        _fft2_kernel,
        grid=(B, g),
        in_specs=[pl.BlockSpec((1, 8, n2, C), lambda b, i: (b, i, 0, 0)),
                  pl.BlockSpec((1, 8, n2, C), lambda b, i: (b, i + g, 0, 0)),
                  pl.BlockSpec((8, 2 * n2, 2 * n2), lambda b, i: (i, 0, 0)),
                  pl.BlockSpec((8, 2 * n2, 2 * n2), lambda b, i: (i, 0, 0)),
                  _full(chh.shape), _full(chl.shape), _full(w_f.shape), _full((1, C))],
        out_specs=pl.BlockSpec((1, n2, 8, C), lambda b, i: (b, 0, i, 0)),
        out_shape=jax.ShapeDtypeStruct((B, n2, n1, C), F32),
        compiler_params=_cparams(2),
        name="fnet_stage2",
    )(ab, ab, mh, ml, chh, chl, w_f, b_f.reshape(1, C))
    return y.reshape(B * S, C)


def _fnet_consts(S):
    n1, n2 = DFT_N1, S // DFT_N1
    k1 = np.arange(n1)
    ang1 = 2.0 * np.pi * np.outer(k1, k1) / n1
    c1 = np.concatenate([np.cos(ang1), np.sin(ang1)], axis=0)
    kk = k1[:, None, None] + n1 * np.arange(n2)[None, :, None]
    ang = 2.0 * np.pi * ((kk * np.arange(n2)[None, None, :]) % S) / S
    ct, st = np.cos(ang), np.sin(ang)
    m = np.concatenate([np.concatenate([ct, -st], axis=2),
                        np.concatenate([st, ct], axis=2)], axis=1)
    kc = np.arange(FNET_CH)
    angc = 2.0 * np.pi * np.outer(kc, kc) / FNET_CH
    eye = np.eye(GROUP_W // FNET_CH)
    ch = np.concatenate([np.kron(eye, np.cos(angc)), -np.kron(eye, np.sin(angc))], axis=0)

    def split(a):
        a32 = jnp.asarray(a, F32)
        hi = a32.astype(BF16)
        return hi, (a32 - hi.astype(F32)).astype(BF16)

    return split(c1) + split(m) + split(ch)


def _conv_kernel(prev_ref, cur_ref, next_ref, w_ref, b_ref, g_ref, beta_ref, wpw_ref, bpw_ref, o_ref, ext_ref):
    i = pl.program_id(1)
    ts = cur_ref.shape[1]
    H = CONV_HALO
    ext_ref[0:H, :] = jnp.where(i > 0, prev_ref[0], 0.0)
    ext_ref[H:H + ts, :] = cur_ref[0]
    ext_ref[H + ts:H + ts + H, :] = jnp.where(i < pl.num_programs(1) - 1, next_ref[0], 0.0)
    acc = jnp.zeros((ts, cur_ref.shape[2]), F32)
    for k in range(CONV_K):
        off = H - CONV_K // 2 + k
        acc = acc + ext_ref[off:off + ts, :] * w_ref[k:k + 1, :]
    y = _layer_norm(acc + b_ref[...], g_ref[...], beta_ref[...])
    y = y * (1.0 / (1.0 + jnp.exp(-y)))
    o_ref[...] = _dot(y.astype(BF16), wpw_ref[...]) + bpw_ref[...]


def _conv_call(u, B, S, dw_w, dw_b, cg, cb, w_pw, b_pw, ts=512):
    C = GROUP_W
    H = CONV_HALO
    nb = S // ts
    r = ts // H
    u3 = u.reshape(B, S, C)
    vec = lambda a: a.reshape(1, C)
    return pl.pallas_call(
        _conv_kernel,
        grid=(B, nb),
        in_specs=[pl.BlockSpec((1, H, C), lambda b, i: (b, jnp.maximum(i * r - 1, 0), 0)),
                  pl.BlockSpec((1, ts, C), lambda b, i: (b, i, 0)),
                  pl.BlockSpec((1, H, C), lambda b, i: (b, jnp.minimum((i + 1) * r, S // H - 1), 0)),
                  _full((CONV_K, C)), _full((1, C)), _full((1, C)), _full((1, C)), _full((C, C)), _full((1, C))],
        out_specs=pl.BlockSpec((ts, C), lambda b, i: (b * nb + i, 0)),
        out_shape=jax.ShapeDtypeStruct((B * S, C), F32),
        scratch_shapes=[pltpu.VMEM((ts + 2 * H, C), F32)],
        compiler_params=_cparams(2),
        name="conv_module",
    )(u3, u3, u3, dw_w, vec(dw_b), vec(cg), vec(cb), w_pw, vec(b_pw))


def _attn_kernel(q_ref, k_ref, v_ref, o_ref, *, k_tile, v_tile, tk):
    tq = q_ref.shape[1]
    S = k_ref.shape[1]
    dv = GQA_HEAD_DIM
    outs = []
    for h in range(len(k_tile)):
        q = q_ref[0, :, h * LANE:(h + 1) * LANE]
        kb, vb = k_tile[h], v_tile[h]

        def body(c, carry):
            m, acc = carry
            off = pl.multiple_of(c * tk, tk)
            k = k_ref[0, pl.ds(off, tk), kb * LANE:(kb + 1) * LANE]
            v = v_ref[0, pl.ds(off, tk), vb * LANE:(vb + 1) * LANE]
            s = lax.dot_general(q, k, (((1,), (1,)), ((), ())), preferred_element_type=F32)
            m_new = jnp.maximum(m, jnp.max(s, axis=-1, keepdims=True))
            p = jnp.exp(s - m_new)
            acc = jnp.exp(m - m_new) * acc + _dot(p.astype(BF16), v)
            return m_new, acc

        m0 = jnp.full((tq, 1), NEG_BIG, F32)
        acc0 = jnp.zeros((tq, LANE), F32)
        _, acc = lax.fori_loop(0, S // tk, body, (m0, acc0))
        outs.append(acc[:, :dv] / acc[:, dv:dv + 1])
    o_ref[0] = jnp.concatenate(outs, axis=1)


def _attn_call(q, k, v, B, S, k_tile, v_tile, name, tq=512, tk=512):
    nh = len(k_tile)
    q3 = q.reshape(B, S, q.shape[-1])
    k3 = k.reshape(B, S, k.shape[-1])
    v3 = v.reshape(B, S, v.shape[-1])
    kern = functools.partial(_attn_kernel, k_tile=k_tile, v_tile=v_tile, tk=tk)
    out = pl.pallas_call(
        kern,
        grid=(B, S // tq),
        in_specs=[pl.BlockSpec((1, tq, q3.shape[-1]), lambda b, i: (b, i, 0)),
                  pl.BlockSpec((1, S, k3.shape[-1]), lambda b, i: (b, 0, 0)),
                  pl.BlockSpec((1, S, v3.shape[-1]), lambda b, i: (b, 0, 0))],
        out_specs=pl.BlockSpec((1, tq, nh * GQA_HEAD_DIM), lambda b, i: (b, i, 0)),
        out_shape=jax.ShapeDtypeStruct((B, S, nh * GQA_HEAD_DIM), F32),
        compiler_params=_cparams(2),
        name=name,
    )(q3, k3, v3)
    return out.reshape(B * S, nh * GQA_HEAD_DIM)


def _mixout_kernel(yf_ref, yc_ref, yg_ref, ym_ref, h_ref, gg_ref, wo_ref, g_ref, b_ref, o_ref):
    parts = [_rms(r[...], gg_ref[i:i + 1, :]).astype(BF16)
             for i, r in enumerate((yf_ref, yc_ref, yg_ref, ym_ref))]
    mix = _dot(jnp.concatenate(parts, axis=1), wo_ref[...])
    o_ref[...] = _layer_norm(DN_ALPHA * h_ref[...] + mix, g_ref[...], b_ref[...])


def _mixout_call(ys, h, grp_g, w_o, g, b, tm=512):
    T, D = h.shape
    C = GROUP_W
    row = lambda w: pl.BlockSpec((tm, w), lambda i: (i, 0))
    return pl.pallas_call(
        _mixout_kernel,
        grid=(T // tm,),
        in_specs=[row(C)] * 4 + [row(D), _full((N_GROUPS, C)), _full((D, D)), _full((1, D)), _full((1, D))],
        out_specs=row(D),
        out_shape=jax.ShapeDtypeStruct((T, D), F32),
        compiler_params=_cparams(1),
        name="mix_out",
    )(*ys, h, grp_g, w_o, g.reshape(1, D), b.reshape(1, D))


def _memkv_kernel(m_ref, wk_ref, wv_ref, k_ref, v_ref):
    m = m_ref[...].astype(BF16)
    k_ref[...] = _dot(m, wk_ref[...]).astype(BF16)
    v_ref[...] = _dot(m, wv_ref[...]).astype(BF16)


def _memkv_call(mem2, w_xk, w_xv, tm=256):
    R, D = mem2.shape
    row = pl.BlockSpec((tm, D), lambda i: (i, 0))
    return pl.pallas_call(
        _memkv_kernel,
        grid=(R // tm,),
        in_specs=[row, _full((D, D)), _full((D, D))],
        out_specs=[row, row],
        out_shape=[jax.ShapeDtypeStruct((R, D), BF16)] * 2,
        compiler_params=_cparams(1),
        name="mem_kv",
    )(mem2, w_xk, w_xv)


def _xattn_kernel(h_ref, wq_ref, k_ref, v_ref, wo_ref, g_ref, b_ref, o_ref):
    h = h_ref[0]
    q = (_dot(h.astype(BF16), wq_ref[...]) * (X_HEAD_DIM ** -0.5)).astype(BF16)
    outs = []
    for hd in range(X_HEADS):
        sl = slice(hd * X_HEAD_DIM, (hd + 1) * X_HEAD_DIM)
        s = lax.dot_general(q[:, sl], k_ref[0, :, sl], (((1,), (1,)), ((), ())), preferred_element_type=F32)
        e = jnp.exp(s - jnp.max(s, axis=-1, keepdims=True))
        o = _dot(e.astype(BF16), v_ref[0, :, sl]) / jnp.sum(e, axis=-1, keepdims=True)
        outs.append(o.astype(BF16))
    xa = _dot(jnp.concatenate(outs, axis=1), wo_ref[...])
    o_ref[0] = _layer_norm(DN_ALPHA * h + xa, g_ref[...], b_ref[...])


def _xattn_call(h, B, S, w_xq, kmem, vmem, w_xo, g, b, tm=256):
    D = D_MODEL
    M = kmem.shape[0] // B
    row = pl.BlockSpec((1, tm, D), lambda bb, i: (bb, i, 0))
    kv = pl.BlockSpec((1, M, D), lambda bb, i: (bb, 0, 0))
    out = pl.pallas_call(
        _xattn_kernel,
        grid=(B, S // tm),
        in_specs=[row, _full((D, D)), kv, kv, _full((D, D)), _full((1, D)), _full((1, D))],
        out_specs=row,
        out_shape=jax.ShapeDtypeStruct((B, S, D), F32),
        compiler_params=_cparams(2),
        name="cross_attn",
    )(h.reshape(B, S, D), w_xq, kmem.reshape(B, M, D), vmem.reshape(B, M, D), w_xo, g.reshape(1, D), b.reshape(1, D))
    return out.reshape(B * S, D)


def _router_kernel(h_ref, whi_ref, wlo_ref, b_ref, idx_ref, gate_ref):
    hh, hl = _split(h_ref[...])
    logits = _dot3(hh, hl, whi_ref[...], wlo_ref[...]) + b_ref[...]
    lane = lax.broadcasted_iota(jnp.int32, logits.shape, 1)
    idx_out = jnp.zeros(logits.shape, jnp.int32)
    val_out = jnp.zeros(logits.shape, F32)
    top = None
    denom = jnp.zeros((logits.shape[0], 1), F32)
    for k in range(TOP_K):
        mx = jnp.max(logits, axis=-1, keepdims=True)
        pick = jnp.min(jnp.where(logits == mx, lane, LANE), axis=-1, keepdims=True)
        if k == 0:
            top = mx
        e = jnp.exp(mx - top)
        denom = denom + e
        idx_out = jnp.where(lane == k, pick, idx_out)
        val_out = jnp.where(lane == k, e, val_out)
        logits = jnp.where(lane == pick, NEG_BIG * 2.0, logits)
    idx_ref[...] = idx_out
    gate_ref[...] = val_out / denom


def _router_call(h, w_hi, w_lo, b_pad, tm=1024):
    T, D = h.shape
    row = lambda w: pl.BlockSpec((tm, w), lambda i: (i, 0))
    return pl.pallas_call(
        _router_kernel,
        grid=(T // tm,),
        in_specs=[row(D), _full((D, LANE)), _full((D, LANE)), _full((1, LANE))],
        out_specs=[row(LANE), row(LANE)],
        out_shape=[jax.ShapeDtypeStruct((T, LANE), jnp.int32), jax.ShapeDtypeStruct((T, LANE), F32)],
        compiler_params=_cparams(1),
        name="router",
    )(h, w_hi, w_lo, b_pad)


def _moe_kernel(be_ref, nv_ref, src_ref, dst_ref, h_hbm, wgu_ref, bgu_ref, wdn_ref, bdn_ref, y_hbm,
                xbuf, ybuf, wgu_bf, wdn_bf, gsem, ssem):
    j = pl.program_id(0)
    prev_e = be_ref[jnp.maximum(j - 1, 0)]

    @pl.when(j == 0)
    def _():
        ybuf[...] = jnp.zeros(ybuf.shape, F32)
        dump = pltpu.make_async_copy(ybuf, y_hbm.at[pl.ds(y_hbm.shape[0] - MOE_BLOCK, MOE_BLOCK)], ssem)
        dump.start()
        dump.wait()

    @pl.when((j == 0) | (be_ref[j] != prev_e))
    def _():
        wgu_bf[...] = wgu_ref[0].astype(BF16)
        wdn_bf[...] = wdn_ref[0].astype(BF16)

    @pl.when(nv_ref[j] > 0)
    def _():
        def gather(r, _):
            pltpu.make_async_copy(h_hbm.at[pl.ds(src_ref[0, 0, r], 1)], xbuf.at[pl.ds(r, 1)], gsem).start()
            return 0

        lax.fori_loop(0, MOE_BLOCK, gather, 0)
        pltpu.make_async_copy(h_hbm.at[pl.ds(0, MOE_BLOCK)], xbuf, gsem).wait()

        gu = _dot(xbuf[...].astype(BF16), wgu_bf[...]) + bgu_ref[0]
        gate = jnp.minimum(gu[:, :D_FF], SWIGLU_LIMIT)
        up = jnp.clip(gu[:, D_FF:], -SWIGLU_LIMIT, SWIGLU_LIMIT)
        act = (up + 1.0) * (gate * (1.0 / (1.0 + jnp.exp(-SWIGLU_ALPHA * gate))))
        ybuf[...] = _dot(act.astype(BF16), wdn_bf[...]) + bdn_ref[0]

        def scatter(r, _):
            pltpu.make_async_copy(ybuf.at[pl.ds(r, 1)], y_hbm.at[pl.ds(dst_ref[0, 0, r], 1)], ssem).start()
            return 0

        lax.fori_loop(0, MOE_BLOCK, scatter, 0)
        pltpu.make_async_copy(ybuf, y_hbm.at[pl.ds(0, MOE_BLOCK)], ssem).wait()


def _moe_call(h, block_e, block_nv, slot_src, slot_dst, w_gu, b_gu, w_down, b_down):
    T, D = h.shape
    nb = block_e.shape[0]
    E = w_gu.shape[0]
    smem_blk = pl.BlockSpec((1, 1, MOE_BLOCK), lambda j, be, nv: (j, 0, 0), memory_space=pltpu.SMEM)
    by_e = lambda shp: pl.BlockSpec((1,) + shp, lambda j, be, nv: (be[j], 0, 0))
    grid_spec = pltpu.PrefetchScalarGridSpec(
        num_scalar_prefetch=2,
        grid=(nb,),
        in_specs=[smem_blk, smem_blk, pl.BlockSpec(memory_space=pl.ANY),
                  by_e((D, 2 * D_FF)), by_e((1, 2 * D_FF)), by_e((D_FF, D)), by_e((1, D))],
        out_specs=pl.BlockSpec(memory_space=pl.ANY),
        scratch_shapes=[pltpu.VMEM((MOE_BLOCK, D), F32), pltpu.VMEM((MOE_BLOCK, D), F32),
                        pltpu.VMEM((D, 2 * D_FF), BF16), pltpu.VMEM((D_FF, D), BF16),
                        pltpu.SemaphoreType.DMA(()), pltpu.SemaphoreType.DMA(())],
    )
    return pl.pallas_call(
        _moe_kernel,
        grid_spec=grid_spec,
        out_shape=jax.ShapeDtypeStruct((T * TOP_K + MOE_BLOCK, D), F32),
        compiler_params=_cparams(1),
        name="routed_experts",
    )(block_e, block_nv, slot_src.reshape(nb, 1, MOE_BLOCK), slot_dst.reshape(nb, 1, MOE_BLOCK),
      h, w_gu, b_gu.reshape(E, 1, 2 * D_FF), w_down, b_down.reshape(E, 1, D))


def _moe_plan(top_i, T):
    N = T * TOP_K
    nb = N // MOE_BLOCK + N_EXPERTS
    P = nb * MOE_BLOCK
    tk_id = jnp.arange(N, dtype=jnp.int32).reshape(T, TOP_K)
    skey = jnp.sort((top_i * N + tk_id).reshape(N))
    bounds = jnp.searchsorted(skey, jnp.arange(N_EXPERTS + 1, dtype=jnp.int32) * N).astype(jnp.int32)
    start, counts = bounds[:-1], bounds[1:] - bounds[:-1]
    nblk = (counts + MOE_BLOCK - 1) // MOE_BLOCK
    bend = jnp.cumsum(nblk)
    bstart = bend - nblk
    blk = jnp.arange(nb, dtype=jnp.int32)
    block_e = jnp.minimum(jnp.searchsorted(bend, blk, side="right"), N_EXPERTS - 1).astype(jnp.int32)
    slot = jnp.arange(P, dtype=jnp.int32)
    se = block_e[slot // MOE_BLOCK]
    r = slot - bstart[se] * MOE_BLOCK
    valid = (r >= 0) & (r < counts[se])
    tk = skey[jnp.clip(start[se] + r, 0, N - 1)] % N
    slot_src = jnp.where(valid, tk // TOP_K, 0).astype(jnp.int32)
    slot_dst = jnp.where(valid, tk, N + slot % MOE_BLOCK).astype(jnp.int32)
    block_nv = jnp.sum(valid.reshape(nb, MOE_BLOCK), axis=1).astype(jnp.int32)
    return block_e, block_nv, slot_src, slot_dst


def _combine_kernel(y_ref, gate_ref, h_ref, g_ref, b_ref, o_ref):
    D = h_ref.shape[1]
    gates = gate_ref[...]
    ff = jnp.zeros(h_ref.shape, F32)
    for k in range(TOP_K):
        ff = ff + gates[:, k:k + 1] * y_ref[:, k * D:(k + 1) * D]
    o_ref[...] = _layer_norm(DN_ALPHA * h_ref[...] + ff, g_ref[...], b_ref[...])


def _combine_call(ys, gates, h, g, b, tm=256):
    T, D = h.shape
    y2 = ys.reshape(ys.shape[0] // TOP_K, TOP_K * D)
    row = lambda w: pl.BlockSpec((tm, w), lambda i: (i, 0))
    return pl.pallas_call(
        _combine_kernel,
        grid=(T // tm,),
        in_specs=[row(TOP_K * D), row(LANE), row(D), _full((1, D)), _full((1, D))],
        out_specs=row(D),
        out_shape=jax.ShapeDtypeStruct((T, D), F32),
        compiler_params=_cparams(1),
        name="combine",
    )(y2, gates, h, g.reshape(1, D), b.reshape(1, D))


def _rope_tables(S):
    pos = np.arange(S)
    row, col = pos // GRID_W, pos % GRID_W

    def tab(p, dim):
        inv = ROPE_THETA ** (-np.arange(0, dim, 2, dtype=np.float64) / dim)
        ang = p[:, None].astype(np.float64) * inv[None, :]
        return np.cos(ang), np.sin(ang)

    def pair(dim):
        cr, sr = tab(row, dim)
        cc, sc = tab(col, dim)
        return (np.concatenate([cr, cr, cc, cc], axis=1), np.concatenate([-sr, sr, -sc, sc], axis=1))

    cg, sg = pair(GQA_HEAD_DIM // 2)
    cg, sg = np.tile(cg, (1, 2)), np.tile(sg, (1, 2))
    cm32, sm32 = pair(MLA_ROPE // 2)
    cm = np.ones((S, LANE))
    sm = np.zeros((S, LANE))
    cm[:, MLA_NOPE:MLA_NOPE + MLA_ROPE] = cm32
    sm[:, MLA_NOPE:MLA_NOPE + MLA_ROPE] = sm32
    return tuple(jnp.asarray(a, F32) for a in (cg, sg, cm, sm))


def _pad_heads(w, n_heads, width, lo, hi):
    r = w.shape[0]
    w3 = w.reshape(r, n_heads, width)[:, :, lo:hi]
    w3 = jnp.pad(w3, ((0, 0), (0, 0), (0, HEAD_PAD - (hi - lo))))
    return w3.reshape(r, n_heads * HEAD_PAD).astype(BF16)


def kernel(x, mem, ln_in_g, ln_in_b, w_in, w_f, b_f, dw_w, dw_b, conv_ln_g, conv_ln_b, w_pw, b_pw, q_norm_g, k_norm_g, mla_q_norm_g, w_uq, mla_kv_norm_g, w_ukv, grp_norm_g, w_o, ln1_g, ln1_b, w_xq, w_xk, w_xv, w_xo, ln2_g, ln2_b, w_router, b_router, w_gu, b_gu, w_down, b_down, ln3_g, ln3_b):
    B, S, D = x.shape
    T = B * S
    depth = w_in.shape[0]
    tabs = _rope_tables(S)
    fconsts = _fnet_consts(S)
    lane = np.arange(LANE)
    gsum = jnp.asarray((lane[:, None] // GQA_HEAD_DIM) == (lane[None, :] // GQA_HEAD_DIM), BF16)
    mem2 = mem.reshape(B * mem.shape[1], D)
    kr_lo = w_in.shape[2] - MLA_ROPE

    h = _ln_call(x.reshape(T, D), ln_in_g, ln_in_b)
    for l in range(depth):
        zeros = lambda n: jnp.zeros((D, n), F32)
        w_in_p = jnp.concatenate([w_in[l][:, :kr_lo], zeros(MLA_NOPE), w_in[l][:, kr_lo:],
                                  zeros(LANE - MLA_NOPE - MLA_ROPE)], axis=1).astype(BF16)
        wuq = _pad_heads(w_uq[l], MLA_HEADS, MLA_NOPE + MLA_ROPE, 0, MLA_NOPE + MLA_ROPE)
        wukn = _pad_heads(w_ukv[l], MLA_HEADS, MLA_NOPE + MLA_V, 0, MLA_NOPE)
        wuv = _pad_heads(w_ukv[l], MLA_HEADS, MLA_NOPE + MLA_V, MLA_NOPE, MLA_NOPE + MLA_V)
        qg = jnp.tile(q_norm_g[l], 2).reshape(1, LANE)
        kg = jnp.tile(k_norm_g[l], 2).reshape(1, LANE)
        zf, u, qgq, kgk, vgv, qm, km, vm = _inproj_call(
            h, S, w_in_p, tabs, gsum, qg, kg, mla_q_norm_g[l].reshape(1, -1), mla_kv_norm_g[l].reshape(1, -1),
            wuq, wukn, wuv)

        yf = _fnet_call(zf, B, S, fconsts, w_f[l].astype(BF16), b_f[l])
        yc = _conv_call(u, B, S, dw_w[l], dw_b[l], conv_ln_g[l], conv_ln_b[l], w_pw[l].astype(BF16), b_pw[l])
        yg = _attn_call(qgq, kgk, vgv, B, S, (0, 0, 0, 0), (0, 0, 1, 1), "gqa_attn")
        ym = _attn_call(qm, km, vm, B, S, (0, 1, 2, 3), (0, 1, 2, 3), "mla_attn")
        h = _mixout_call((yf, yc, yg, ym), h, grp_norm_g[l], w_o[l].astype(BF16), ln1_g[l], ln1_b[l])

        kmem, vmem = _memkv_call(mem2, w_xk[l].astype(BF16), w_xv[l].astype(BF16))
        h = _xattn_call(h, B, S, w_xq[l].astype(BF16), kmem, vmem, w_xo[l].astype(BF16), ln2_g[l], ln2_b[l])

        w_r = jnp.pad(w_router[l], ((0, 0), (0, LANE - N_EXPERTS)))
        w_r_hi = w_r.astype(BF16)
        w_r_lo = (w_r - w_r_hi.astype(F32)).astype(BF16)
        b_r = jnp.concatenate([b_router[l], jnp.full((LANE - N_EXPERTS,), NEG_BIG, F32)]).reshape(1, LANE)
        idx, gates = _router_call(h, w_r_hi, w_r_lo, b_r)
        block_e, block_nv, slot_src, slot_dst = _moe_plan(idx[:, :TOP_K], T)
        ys = _moe_call(h, block_e, block_nv, slot_src, slot_dst, w_gu[l], b_gu[l], w_down[l], b_down[l])
        h = _combine_call(ys, gates, h, ln3_g[l], ln3_b[l])
    return h.reshape(B, S, D)
```

```python
import functools
import math

import numpy as np
import jax
import jax.numpy as jnp
from jax import lax
from jax.experimental import pallas as pl
from jax.experimental.pallas import tpu as pltpu

F32 = jnp.float32
BF16 = jnp.bfloat16

D_MODEL = 1024
DEPTH = 2
GRID_W = 64
N_GROUPS = 4
GROUP_W = D_MODEL // N_GROUPS
FNET_CH = 64
CONV_K = 31
CONV_HALO = 16
GQA_HEAD_DIM = 64
GQA_Q_HEADS = 4
GQA_KV_HEADS = 2
MLA_HEADS = 4
MLA_NOPE = 64
MLA_ROPE = 32
MLA_V = 64
MLA_Q_RANK = 256
MLA_KV_RANK = 128
X_HEADS = 4
X_HEAD_DIM = D_MODEL // X_HEADS
N_EXPERTS = 32
TOP_K = 4
D_FF = D_MODEL
SWIGLU_LIMIT = 7.0
SWIGLU_ALPHA = 1.702
ROPE_THETA = 10000.0
LN_EPS = 1e-5
RMS_EPS = 1e-6
DN_ALPHA = (2 * DEPTH) ** 0.25

LANE = 128
HEAD_PAD = 128
IN_COLS_PAD = 1792
DFT_N1 = 64
MOE_BLOCK = 256
NEG_BIG = -1e30
VMEM_LIMIT = 56 * 1024 * 1024


def _cparams(n_axes, vmem=VMEM_LIMIT):
    return pltpu.CompilerParams(dimension_semantics=("arbitrary",) * n_axes, vmem_limit_bytes=vmem)


def _full(shape):
    nd = len(shape)
    return pl.BlockSpec(shape, lambda *a: (0,) * nd)


def _dot(a, b):
    return jnp.dot(a, b, preferred_element_type=F32)


def _split(x):
    hi = x.astype(BF16)
    lo = (x - hi.astype(F32)).astype(BF16)
    return hi, lo


def _dot3(a_hi, a_lo, b_hi, b_lo):
    return _dot(a_hi, b_hi) + (_dot(a_hi, b_lo) + _dot(a_lo, b_hi))


def _layer_norm(x, g, b):
    mu = jnp.mean(x, axis=-1, keepdims=True)
    xc = x - mu
    var = jnp.mean(xc * xc, axis=-1, keepdims=True)
    return xc * lax.rsqrt(var + LN_EPS) * g + b


def _rms(x, g):
    return x * lax.rsqrt(jnp.mean(x * x, axis=-1, keepdims=True) + RMS_EPS) * g


def _rope(x, cos, sin_signed, half):
    lane = lax.broadcasted_iota(jnp.int32, x.shape, 1)
    first = (lane % (2 * half)) < half
    partner = jnp.where(first, pltpu.roll(x, LANE - half, 1), pltpu.roll(x, half, 1))
    return x * cos + partner * sin_signed


def _ln_kernel(x_ref, g_ref, b_ref, o_ref):
    o_ref[...] = _layer_norm(x_ref[...], g_ref[...], b_ref[...])


def _ln_call(x, g, b, tm=1024):
    T, D = x.shape
    return pl.pallas_call(
        _ln_kernel,
        grid=(T // tm,),
        in_specs=[pl.BlockSpec((tm, D), lambda i: (i, 0)), _full((1, D)), _full((1, D))],
        out_specs=pl.BlockSpec((tm, D), lambda i: (i, 0)),
        out_shape=jax.ShapeDtypeStruct((T, D), F32),
        compiler_params=_cparams(1),
        name="ln_in",
    )(x, g.reshape(1, D), b.reshape(1, D))


def _inproj_kernel(h_ref, w_ref, cg_ref, sg_ref, cm_ref, sm_ref, gsum_ref, qg_ref, kg_ref,
                   mqg_ref, mkg_ref, wuq_ref, wukn_ref, wuv_ref,
                   zf_ref, u_ref, qgo_ref, kgo_ref, vgo_ref, qmo_ref, kmo_ref, vmo_ref):
    z = _dot(h_ref[...].astype(BF16), w_ref[...])
    zf_ref[...] = z[:, 0:256]
    a = z[:, 256:512]
    g = z[:, 512:768]
    u_ref[...] = a * (1.0 / (1.0 + jnp.exp(-g)))

    cg = cg_ref[...]
    sg = sg_ref[...]
    cm = cm_ref[...]
    sm = sm_ref[...]
    lane = lax.broadcasted_iota(jnp.int32, cg.shape, 1)
    low = lane < GQA_HEAD_DIM
    gsum = gsum_ref[...]

    def head_norm(x, gain):
        hi, lo = _split(x * x)
        ssq = _dot(hi, gsum) + _dot(lo, gsum)
        return x * lax.rsqrt(ssq * (1.0 / GQA_HEAD_DIM) + RMS_EPS) * gain

    scale_g = GQA_HEAD_DIM ** -0.5
    q_tiles = []
    for t in range(2):
        x = z[:, 768 + t * LANE: 768 + (t + 1) * LANE]
        x = _rope(head_norm(x, qg_ref[...]), cg, sg, GQA_HEAD_DIM // 4) * scale_g
        xr = pltpu.roll(x, GQA_HEAD_DIM, 1)
        if t == 0:
            q_tiles += [jnp.where(low, x, 0.0), jnp.where(low, xr, 0.0)]
        else:
            q_tiles += [jnp.where(low, 0.0, xr), jnp.where(low, 0.0, x)]
    qgo_ref[...] = jnp.concatenate(q_tiles, axis=1).astype(BF16)

    k = z[:, 1024:1152]
    kgo_ref[...] = _rope(head_norm(k, kg_ref[...]), cg, sg, GQA_HEAD_DIM // 4).astype(BF16)

    v = z[:, 1152:1280]
    ones_col = jnp.where(lane == GQA_HEAD_DIM, 1.0, 0.0)
    v0 = jnp.where(low, v, ones_col)
    v1 = jnp.where(low, pltpu.roll(v, GQA_HEAD_DIM, 1), ones_col)
    vgo_ref[...] = jnp.concatenate([v0, v1], axis=1).astype(BF16)

    scale_m = (MLA_NOPE + MLA_ROPE) ** -0.5
    cq = _rms(z[:, 1280:1536], mqg_ref[...]).astype(BF16)
    qm = _dot(cq, wuq_ref[...])
    qm_tiles = [_rope(qm[:, t * LANE:(t + 1) * LANE], cm, sm, MLA_ROPE // 4) * scale_m
                for t in range(MLA_HEADS)]
    qmo_ref[...] = jnp.concatenate(qm_tiles, axis=1).astype(BF16)

    ckv = _rms(z[:, 1536:1664], mkg_ref[...]).astype(BF16)
    kr = _rope(z[:, 1664:1792], cm, sm, MLA_ROPE // 4)
    kn = _dot(ckv, wukn_ref[...])
    kmo_ref[...] = (kn + jnp.concatenate([kr] * MLA_HEADS, axis=1)).astype(BF16)
    vm = _dot(ckv, wuv_ref[...])
    vmo_ref[...] = (vm + jnp.concatenate([ones_col] * MLA_HEADS, axis=1)).astype(BF16)


def _inproj_call(h, S, w_in_p, tabs, gsum, qg, kg, mqg, mkg, wuq, wukn, wuv, tm=512):
    T, D = h.shape
    nsb = S // tm
    cg, sg, cm, sm = tabs
    row = lambda w: pl.BlockSpec((tm, w), lambda i: (i, 0))
    tab = pl.BlockSpec((tm, LANE), lambda i: (i % nsb, 0))
    outs = [(256, F32), (256, F32), (512, BF16), (128, BF16), (256, BF16), (512, BF16), (512, BF16), (512, BF16)]
    return pl.pallas_call(
        _inproj_kernel,
        grid=(T // tm,),
        in_specs=[row(D), _full(w_in_p.shape), tab, tab, tab, tab, _full(gsum.shape), _full(qg.shape),
                  _full(kg.shape), _full(mqg.shape), _full(mkg.shape), _full(wuq.shape), _full(wukn.shape),
                  _full(wuv.shape)],
        out_specs=[row(w) for w, _ in outs],
        out_shape=[jax.ShapeDtypeStruct((T, w), dt) for w, dt in outs],
        compiler_params=_cparams(1),
        name="in_proj",
    )(h, w_in_p, cg, sg, cm, sm, gsum, qg, kg, mqg, mkg, wuq, wukn, wuv)


def _fft1_kernel(x_ref, chi_ref, clo_ref, o_ref):
    chi = chi_ref[...]
    clo = clo_ref[...]
    for j in range(8):
        xh, xl = _split(x_ref[0, :, j, :])
        o_ref[0, :, j, :] = _dot3(chi, clo, xh, xl)


def _fft2_kernel(re_ref, im_ref, mhi_ref, mlo_ref, chh_ref, chl_ref, wf_ref, bf_ref, o_ref):
    n2 = re_ref.shape[2]
    for j in range(8):
        ab = jnp.concatenate([re_ref[0, j], im_ref[0, j]], axis=0)
        abh, abl = _split(ab)
        ri = _dot3(mhi_ref[j], mlo_ref[j], abh, abl)
        rc = jnp.concatenate([ri[:n2], ri[n2:]], axis=1)
        rch, rcl = _split(rc)
        y = _dot3(rch, rcl, chh_ref[...], chl_ref[...])
        o_ref[0, :, j, :] = _dot(y.astype(BF16), wf_ref[...]) + bf_ref[...]


def _fnet_call(zf, B, S, consts, w_f, b_f):
    C = GROUP_W
    n1, n2 = DFT_N1, S // DFT_N1
    c1h, c1l, mh, ml, chh, chl = consts
    x4 = zf.reshape(B, n1, n2, C)
    ab = pl.pallas_call(
        _fft1_kernel,
        grid=(B, n2 // 8),
        in_specs=[pl.BlockSpec((1, n1, 8, C), lambda b, i: (b, 0, i, 0)), _full(c1h.shape), _full(c1l.shape)],
        out_specs=pl.BlockSpec((1, 2 * n1, 8, C), lambda b, i: (b, 0, i, 0)),
        out_shape=jax.ShapeDtypeStruct((B, 2 * n1, n2, C), F32),
        compiler_params=_cparams(2),
        name="fnet_stage1",
    )(x4, c1h, c1l)
    g = n1 // 8
    y = pl.pallas_call(
        _fft2_kernel,
        grid=(B, g),
        in_specs=[pl.BlockSpec((1, 8, n2, C), lambda b, i: (b, i, 0, 0)),
                  pl.BlockSpec((1, 8, n2, C), lambda b, i: (b, i + g, 0, 0)),
                  pl.BlockSpec((8, 2 * n2, 2 * n2), lambda b, i: (i, 0, 0)),
                  pl.BlockSpec((8, 2 * n2, 2 * n2), lambda b, i: (i, 0, 0)),
                  _full(chh.shape), _full(chl.shape), _full(w_f.shape), _full((1, C))],
        out_specs=pl.BlockSpec((1, n2, 8, C), lambda b, i: (b, 0, i, 0)),
        out_shape=jax.ShapeDtypeStruct((B, n2, n1, C), F32),
        compiler_params=_cparams(2),
        name="fnet_stage2",
    )(ab, ab, mh, ml, chh, chl, w_f, b_f.reshape(1, C))
    return y.reshape(B * S, C)


def _fnet_consts(S):
    n1, n2 = DFT_N1, S // DFT_N1
    k1 = np.arange(n1)
    ang1 = 2.0 * np.pi * np.outer(k1, k1) / n1
    c1 = np.concatenate([np.cos(ang1), np.sin(ang1)], axis=0)
    kk = k1[:, None, None] + n1 * np.arange(n2)[None, :, None]
    ang = 2.0 * np.pi * ((kk * np.arange(n2)[None, None, :]) % S) / S
    ct, st = np.cos(ang), np.sin(ang)
    m = np.concatenate([np.concatenate([ct, -st], axis=2),
                        np.concatenate([st, ct], axis=2)], axis=1)
    kc = np.arange(FNET_CH)
    angc = 2.0 * np.pi * np.outer(kc, kc) / FNET_CH
    eye = np.eye(GROUP_W // FNET_CH)
    ch = np.concatenate([np.kron(eye, np.cos(angc)), -np.kron(eye, np.sin(angc))], axis=0)

    def split(a):
        a32 = jnp.asarray(a, F32)
        hi = a32.astype(BF16)
        return hi, (a32 - hi.astype(F32)).astype(BF16)

    return split(c1) + split(m) + split(ch)


def _conv_kernel(prev_ref, cur_ref, next_ref, w_ref, b_ref, g_ref, beta_ref, wpw_ref, bpw_ref, o_ref, ext_ref):
    i = pl.program_id(1)
    ts = cur_ref.shape[1]
    H = CONV_HALO
    ext_ref[0:H, :] = jnp.where(i > 0, prev_ref[0], 0.0)
    ext_ref[H:H + ts, :] = cur_ref[0]
    ext_ref[H + ts:H + ts + H, :] = jnp.where(i < pl.num_programs(1) - 1, next_ref[0], 0.0)
    acc = jnp.zeros((ts, cur_ref.shape[2]), F32)
    for k in range(CONV_K):
        off = H - CONV_K // 2 + k
        acc = acc + ext_ref[off:off + ts, :] * w_ref[k:k + 1, :]
    y = _layer_norm(acc + b_ref[...], g_ref[...], beta_ref[...])
    y = y * (1.0 / (1.0 + jnp.exp(-y)))
    o_ref[...] = _dot(y.astype(BF16), wpw_ref[...]) + bpw_ref[...]


def _conv_call(u, B, S, dw_w, dw_b, cg, cb, w_pw, b_pw, ts=512):
    C = GROUP_W
    H = CONV_HALO
    nb = S // ts
    r = ts // H
    u3 = u.reshape(B, S, C)
    vec = lambda a: a.reshape(1, C)
    return pl.pallas_call(
        _conv_kernel,
        grid=(B, nb),
        in_specs=[pl.BlockSpec((1, H, C), lambda b, i: (b, jnp.maximum(i * r - 1, 0), 0)),
                  pl.BlockSpec((1, ts, C), lambda b, i: (b, i, 0)),
                  pl.BlockSpec((1, H, C), lambda b, i: (b, jnp.minimum((i + 1) * r, S // H - 1), 0)),
                  _full((CONV_K, C)), _full((1, C)), _full((1, C)), _full((1, C)), _full((C, C)), _full((1, C))],
        out_specs=pl.BlockSpec((ts, C), lambda b, i: (b * nb + i, 0)),
        out_shape=jax.ShapeDtypeStruct((B * S, C), F32),
        scratch_shapes=[pltpu.VMEM((ts + 2 * H, C), F32)],
        compiler_params=_cparams(2),
        name="conv_module",
    )(u3, u3, u3, dw_w, vec(dw_b), vec(cg), vec(cb), w_pw, vec(b_pw))


def _attn_kernel(q_ref, k_ref, v_ref, o_ref, *, k_tile, v_tile, tk):
    tq = q_ref.shape[1]
    S = k_ref.shape[1]
    dv = GQA_HEAD_DIM
    outs = []
    for h in range(len(k_tile)):
        q = q_ref[0, :, h * LANE:(h + 1) * LANE]
        kb, vb = k_tile[h], v_tile[h]

        def body(c, carry):
            m, acc = carry
            off = pl.multiple_of(c * tk, tk)
            k = k_ref[0, pl.ds(off, tk), kb * LANE:(kb + 1) * LANE]
            v = v_ref[0, pl.ds(off, tk), vb * LANE:(vb + 1) * LANE]
            s = lax.dot_general(q, k, (((1,), (1,)), ((), ())), preferred_element_type=F32)
            m_new = jnp.maximum(m, jnp.max(s, axis=-1, keepdims=True))
            p = jnp.exp(s - m_new)
            acc = jnp.exp(m - m_new) * acc + _dot(p.astype(BF16), v)
            return m_new, acc

        m0 = jnp.full((tq, 1), NEG_BIG, F32)
        acc0 = jnp.zeros((tq, LANE), F32)
        _, acc = lax.fori_loop(0, S // tk, body, (m0, acc0))
        outs.append(acc[:, :dv] / acc[:, dv:dv + 1])
    o_ref[0] = jnp.concatenate(outs, axis=1)


def _attn_call(q, k, v, B, S, k_tile, v_tile, name, tq=512, tk=512):
    nh = len(k_tile)
    q3 = q.reshape(B, S, q.shape[-1])
    k3 = k.reshape(B, S, k.shape[-1])
    v3 = v.reshape(B, S, v.shape[-1])
    kern = functools.partial(_attn_kernel, k_tile=k_tile, v_tile=v_tile, tk=tk)
    out = pl.pallas_call(
        kern,
        grid=(B, S // tq),
        in_specs=[pl.BlockSpec((1, tq, q3.shape[-1]), lambda b, i: (b, i, 0)),
                  pl.BlockSpec((1, S, k3.shape[-1]), lambda b, i: (b, 0, 0)),
                  pl.BlockSpec((1, S, v3.shape[-1]), lambda b, i: (b, 0, 0))],
        out_specs=pl.BlockSpec((1, tq, nh * GQA_HEAD_DIM), lambda b, i: (b, i, 0)),
        out_shape=jax.ShapeDtypeStruct((B, S, nh * GQA_HEAD_DIM), F32),
        compiler_params=_cparams(2),
        name=name,
    )(q3, k3, v3)
    return out.reshape(B * S, nh * GQA_HEAD_DIM)


def _mixout_kernel(yf_ref, yc_ref, yg_ref, ym_ref, h_ref, gg_ref, wo_ref, g_ref, b_ref, o_ref):
    parts = [_rms(r[...], gg_ref[i:i + 1, :]).astype(BF16)
             for i, r in enumerate((yf_ref, yc_ref, yg_ref, ym_ref))]
    mix = _dot(jnp.concatenate(parts, axis=1), wo_ref[...])
    o_ref[...] = _layer_norm(DN_ALPHA * h_ref[...] + mix, g_ref[...], b_ref[...])


def _mixout_call(ys, h, grp_g, w_o, g, b, tm=512):
    T, D = h.shape
    C = GROUP_W
    row = lambda w: pl.BlockSpec((tm, w), lambda i: (i, 0))
    return pl.pallas_call(
        _mixout_kernel,
        grid=(T // tm,),
        in_specs=[row(C)] * 4 + [row(D), _full((N_GROUPS, C)), _full((D, D)), _full((1, D)), _full((1, D))],
        out_specs=row(D),
        out_shape=jax.ShapeDtypeStruct((T, D), F32),
        compiler_params=_cparams(1),
        name="mix_out",
    )(*ys, h, grp_g, w_o, g.reshape(1, D), b.reshape(1, D))


def _memkv_kernel(m_ref, wk_ref, wv_ref, k_ref, v_ref):
    m = m_ref[...].astype(BF16)
    k_ref[...] = _dot(m, wk_ref[...]).astype(BF16)
    v_ref[...] = _dot(m, wv_ref[...]).astype(BF16)


def _memkv_call(mem2, w_xk, w_xv, tm=256):
    R, D = mem2.shape
    row = pl.BlockSpec((tm, D), lambda i: (i, 0))
    return pl.pallas_call(
        _memkv_kernel,
        grid=(R // tm,),
        in_specs=[row, _full((D, D)), _full((D, D))],
        out_specs=[row, row],
        out_shape=[jax.ShapeDtypeStruct((R, D), BF16)] * 2,
        compiler_params=_cparams(1),
        name="mem_kv",
    )(mem2, w_xk, w_xv)


def _xattn_kernel(h_ref, wq_ref, k_ref, v_ref, wo_ref, g_ref, b_ref, o_ref, o3_ref):
    h = h_ref[0]
    q = (_dot(h.astype(BF16), wq_ref[...]) * (X_HEAD_DIM ** -0.5)).astype(BF16)
    outs = []
    for hd in range(X_HEADS):
        sl = slice(hd * X_HEAD_DIM, (hd + 1) * X_HEAD_DIM)
        s = lax.dot_general(q[:, sl], k_ref[0, :, sl], (((1,), (1,)), ((), ())), preferred_element_type=F32)
        e = jnp.exp(s - jnp.max(s, axis=-1, keepdims=True))
        o = _dot(e.astype(BF16), v_ref[0, :, sl]) / jnp.sum(e, axis=-1, keepdims=True)
        outs.append(o.astype(BF16))
    xa = _dot(jnp.concatenate(outs, axis=1), wo_ref[...])
    hn = _layer_norm(DN_ALPHA * h + xa, g_ref[...], b_ref[...])
    o_ref[0] = hn
    tm = hn.shape[0]
    ns = D_MODEL // LANE
    for s in range(ns):
        o3_ref[pl.ds(s, tm, stride=ns), :] = hn[:, s * LANE:(s + 1) * LANE]


def _xattn_call(h, B, S, w_xq, kmem, vmem, w_xo, g, b, tm=256):
    D = D_MODEL
    M = kmem.shape[0] // B
    ns = D // LANE
    nsb = S // tm
    row = pl.BlockSpec((1, tm, D), lambda bb, i: (bb, i, 0))
    kv = pl.BlockSpec((1, M, D), lambda bb, i: (bb, 0, 0))
    out, out3 = pl.pallas_call(
        _xattn_kernel,
        grid=(B, nsb),
        in_specs=[row, _full((D, D)), kv, kv, _full((D, D)), _full((1, D)), _full((1, D))],
        out_specs=[row, pl.BlockSpec((tm * ns, LANE), lambda bb, i: (bb * nsb + i, 0))],
        out_shape=[jax.ShapeDtypeStruct((B, S, D), F32), jax.ShapeDtypeStruct((B * S * ns, LANE), F32)],
        compiler_params=_cparams(2),
        name="cross_attn",
    )(h.reshape(B, S, D), w_xq, kmem.reshape(B, M, D), vmem.reshape(B, M, D), w_xo, g.reshape(1, D), b.reshape(1, D))
    return out.reshape(B * S, D), out3.reshape(B * S, ns, LANE)


def _router_kernel(h_ref, whi_ref, wlo_ref, b_ref, idx_ref, gate_ref):
    hh, hl = _split(h_ref[...])
    logits = _dot3(hh, hl, whi_ref[...], wlo_ref[...]) + b_ref[...]
    lane = lax.broadcasted_iota(jnp.int32, logits.shape, 1)
    idx_out = jnp.zeros(logits.shape, F32)
    val_out = jnp.zeros(logits.shape, F32)
    top = None
    denom = jnp.zeros((logits.shape[0], 1), F32)
    for k in range(TOP_K):
        mx = jnp.max(logits, axis=-1, keepdims=True)
        pick = jnp.min(jnp.where(logits == mx, lane, LANE), axis=-1, keepdims=True)
        if k == 0:
            top = mx
        e = jnp.exp(mx - top)
        denom = denom + e
        idx_out = jnp.where(lane == k, pick.astype(F32), idx_out)
        val_out = jnp.where(lane == k, e, val_out)
        logits = jnp.where(lane == pick, NEG_BIG * 2.0, logits)
    idx_ref[...] = jnp.transpose(idx_out)[:8, :].astype(jnp.int32)
    gate_ref[...] = val_out / denom


def _router_call(h, w_hi, w_lo, b_pad, tm=1024):
    T, D = h.shape
    row = lambda w: pl.BlockSpec((tm, w), lambda i: (i, 0))
    return pl.pallas_call(
        _router_kernel,
        grid=(T // tm,),
        in_specs=[row(D), _full((D, LANE)), _full((D, LANE)), _full((1, LANE))],
        out_specs=[pl.BlockSpec((8, tm), lambda i: (0, i)), row(LANE)],
        out_shape=[jax.ShapeDtypeStruct((8, T), jnp.int32), jax.ShapeDtypeStruct((T, LANE), F32)],
        compiler_params=_cparams(1),
        name="router",
    )(h, w_hi, w_lo, b_pad)


MOE_CHUNKS = 4


def _moe_kernel(be_ref, srcn_ref, src0_ref, dstp_ref, dstc_ref, h_hbm, wgu_ref, bgu_ref, wdn_ref, bdn_ref,
                y_hbm, xbuf, ybuf, wgu_bf, wdn_bf, gsem, ssem):
    j = pl.program_id(0)
    last = pl.num_programs(0) - 1

    def gather(tok, r):
        pltpu.make_async_copy(h_hbm.at[tok], xbuf.at[:, r], gsem).start()

    def scatter(row, r):
        pltpu.make_async_copy(ybuf.at[:, r], y_hbm.at[row], ssem).start()

    def wait_gather():
        pltpu.make_async_copy(xbuf, xbuf, gsem).wait()

    def wait_scatter():
        pltpu.make_async_copy(ybuf, ybuf, ssem).wait()

    @pl.when(j == 0)
    def _():
        ybuf[...] = jnp.zeros(ybuf.shape, F32)

        def first(r, _):
            gather(src0_ref[0, 0, r], r)
            return 0

        lax.fori_loop(0, MOE_BLOCK, first, 0)

    @pl.when((j == 0) | (be_ref[j] != be_ref[jnp.maximum(j - 1, 0)]))
    def _():
        wgu_bf[...] = wgu_ref[0].astype(BF16)
        wdn_bf[...] = wdn_ref[0].astype(BF16)

    wait_gather()
    x = jnp.concatenate([xbuf[s] for s in range(D_MODEL // LANE)], axis=1).astype(BF16)
    cw = D_FF // MOE_CHUNKS
    per = MOE_BLOCK // MOE_CHUNKS
    y = None
    for c in range(MOE_CHUNKS):
        for r in range(c * per, (c + 1) * per):
            gather(srcn_ref[0, 0, r], r)
            scatter(dstp_ref[0, 0, r], r)
        gate = _dot(x, wgu_bf[:, c * cw:(c + 1) * cw]) + bgu_ref[0, :, c * cw:(c + 1) * cw]
        up = _dot(x, wgu_bf[:, D_FF + c * cw:D_FF + (c + 1) * cw]) + bgu_ref[0, :, D_FF + c * cw:D_FF + (c + 1) * cw]
        gate = jnp.minimum(gate, SWIGLU_LIMIT)
        up = jnp.clip(up, -SWIGLU_LIMIT, SWIGLU_LIMIT)
        act = (up + 1.0) * (gate * (1.0 / (1.0 + jnp.exp(-SWIGLU_ALPHA * gate))))
        part = _dot(act.astype(BF16), wdn_bf[c * cw:(c + 1) * cw, :])
        y = part if y is None else y + part
    wait_scatter()
    y = y + bdn_ref[0]
    for s in range(D_MODEL // LANE):
        ybuf[s] = y[:, s * LANE:(s + 1) * LANE]

    @pl.when(j == last)
    def _():
        def final(r, _):
            scatter(dstc_ref[0, 0, r], r)
            return 0

        lax.fori_loop(0, MOE_BLOCK, final, 0)
        wait_scatter()
        wait_gather()


def _moe_call(h3, block_e, slot_src, slot_dst, w_gu, b_gu, w_down, b_down):
    T, D = h3.shape[0], D_MODEL
    nb = block_e.shape[0]
    E = w_gu.shape[0]
    smem = lambda f: pl.BlockSpec((1, 1, MOE_BLOCK), lambda j, be: (f(j), 0, 0), memory_space=pltpu.SMEM)
    by_e = lambda shp: pl.BlockSpec((1,) + shp, lambda j, be: (be[j], 0, 0))
    grid_spec = pltpu.PrefetchScalarGridSpec(
        num_scalar_prefetch=1,
        grid=(nb,),
        in_specs=[smem(lambda j: j + 1), smem(lambda j: 0), smem(lambda j: j), smem(lambda j: j + 1),
                  pl.BlockSpec(memory_space=pl.ANY),
                  by_e((D, 2 * D_FF)), by_e((1, 2 * D_FF)), by_e((D_FF, D)), by_e((1, D))],
        out_specs=pl.BlockSpec(memory_space=pl.ANY),
        scratch_shapes=[pltpu.VMEM((D // LANE, MOE_BLOCK, LANE), F32), pltpu.VMEM((D // LANE, MOE_BLOCK, LANE), F32),
                        pltpu.VMEM((D, 2 * D_FF), BF16), pltpu.VMEM((D_FF, D), BF16),
                        pltpu.SemaphoreType.DMA(()), pltpu.SemaphoreType.DMA(())],
    )
    src3 = slot_src.reshape(nb + 1, 1, MOE_BLOCK)
    dst3 = slot_dst.reshape(nb + 1, 1, MOE_BLOCK)
    return pl.pallas_call(
        _moe_kernel,
        grid_spec=grid_spec,
        out_shape=jax.ShapeDtypeStruct((T * TOP_K + MOE_BLOCK, D // LANE, LANE), F32),
        compiler_params=_cparams(1),
        name="routed_experts",
    )(block_e, src3, src3, dst3, dst3, h3, w_gu, b_gu.reshape(E, 1, 2 * D_FF), w_down, b_down.reshape(E, 1, D))


def _moe_plan(idx_t, T):
    N = T * TOP_K
    nb = N // MOE_BLOCK + N_EXPERTS
    tk_id = jnp.arange(T, dtype=jnp.int32)[None, :] * TOP_K + jnp.arange(TOP_K, dtype=jnp.int32)[:, None]
    skey = jnp.sort((idx_t * N + tk_id).reshape(N))
    bounds = jnp.searchsorted(skey, jnp.arange(N_EXPERTS + 1, dtype=jnp.int32) * N).astype(jnp.int32)
    start, counts = bounds[:-1], bounds[1:] - bounds[:-1]
    nblk = (counts + MOE_BLOCK - 1) // MOE_BLOCK
    bend = jnp.cumsum(nblk)
    bstart = bend - nblk
    blk = jnp.arange(nb, dtype=jnp.int32)
    block_e = jnp.minimum(jnp.searchsorted(bend, blk, side="right"), N_EXPERTS - 1).astype(jnp.int32)
    first = (blk - bstart[block_e]) * MOE_BLOCK
    base = jnp.clip(start[block_e] + first, 0, N)
    skey_p = jnp.concatenate([skey, jnp.zeros((MOE_BLOCK,), jnp.int32)])
    win = jax.vmap(lambda s: lax.dynamic_slice(skey_p, (s,), (MOE_BLOCK,)))(base)
    lane = jnp.arange(MOE_BLOCK, dtype=jnp.int32)[None, :]
    valid = (first[:, None] + lane) < counts[block_e][:, None]
    tk = win % N
    tok, kk = tk // TOP_K, tk % TOP_K
    slot_src = jnp.where(valid, tok, 0).astype(jnp.int32)
    slot_dst = jnp.where(valid, kk * T + tok, N + lane).astype(jnp.int32)
    slot_src = jnp.concatenate([slot_src, jnp.zeros((1, MOE_BLOCK), jnp.int32)], axis=0)
    slot_dst = jnp.concatenate([N + lane, slot_dst], axis=0)
    return block_e, slot_src, slot_dst


def _combine_kernel(y0_ref, y1_ref, y2_ref, y3_ref, gate_ref, h_ref, g_ref, b_ref, o_ref):
    tm = h_ref.shape[0]
    ns = D_MODEL // LANE
    gates = gate_ref[...]
    ff = jnp.zeros(h_ref.shape, F32)
    for k, y_ref in enumerate((y0_ref, y1_ref, y2_ref, y3_ref)):
        y = jnp.concatenate([y_ref[pl.ds(s, tm, stride=ns), :] for s in range(ns)], axis=1)
        ff = ff + gates[:, k:k + 1] * y
    o_ref[...] = _layer_norm(DN_ALPHA * h_ref[...] + ff, g_ref[...], b_ref[...])


def _combine_call(ys3, gates, h, g, b, tm=256):
    T, D = h.shape
    nt = T // tm
    ns = D // LANE
    ys = ys3.reshape(ys3.shape[0] * ns, LANE)
    row = lambda w: pl.BlockSpec((tm, w), lambda i: (i, 0))
    yk = lambda k: pl.BlockSpec((tm * ns, LANE), lambda i: (k * nt + i, 0))
    return pl.pallas_call(
        _combine_kernel,
        grid=(nt,),
        in_specs=[yk(k) for k in range(TOP_K)] + [row(LANE), row(D), _full((1, D)), _full((1, D))],
        out_specs=row(D),
        out_shape=jax.ShapeDtypeStruct((T, D), F32),
        compiler_params=_cparams(1),
        name="combine",
    )(ys, ys, ys, ys, gates, h, g.reshape(1, D), b.reshape(1, D))


def _rope_tables(S):
    pos = np.arange(S)
    row, col = pos // GRID_W, pos % GRID_W

    def tab(p, dim):
        inv = ROPE_THETA ** (-np.arange(0, dim, 2, dtype=np.float64) / dim)
        ang = p[:, None].astype(np.float64) * inv[None, :]
        return np.cos(ang), np.sin(ang)

    def pair(dim):
        cr, sr = tab(row, dim)
        cc, sc = tab(col, dim)
        return (np.concatenate([cr, cr, cc, cc], axis=1), np.concatenate([-sr, sr, -sc, sc], axis=1))

    cg, sg = pair(GQA_HEAD_DIM // 2)
    cg, sg = np.tile(cg, (1, 2)), np.tile(sg, (1, 2))
    cm32, sm32 = pair(MLA_ROPE // 2)
    cm = np.ones((S, LANE))
    sm = np.zeros((S, LANE))
    cm[:, MLA_NOPE:MLA_NOPE + MLA_ROPE] = cm32
    sm[:, MLA_NOPE:MLA_NOPE + MLA_ROPE] = sm32
    return tuple(jnp.asarray(a, F32) for a in (cg, sg, cm, sm))


def _pad_heads(w, n_heads, width, lo, hi):
    r = w.shape[0]
    w3 = w.reshape(r, n_heads, width)[:, :, lo:hi]
    w3 = jnp.pad(w3, ((0, 0), (0, 0), (0, HEAD_PAD - (hi - lo))))
    return w3.reshape(r, n_heads * HEAD_PAD).astype(BF16)


def kernel(x, mem, ln_in_g, ln_in_b, w_in, w_f, b_f, dw_w, dw_b, conv_ln_g, conv_ln_b, w_pw, b_pw, q_norm_g, k_norm_g, mla_q_norm_g, w_uq, mla_kv_norm_g, w_ukv, grp_norm_g, w_o, ln1_g, ln1_b, w_xq, w_xk, w_xv, w_xo, ln2_g, ln2_b, w_router, b_router, w_gu, b_gu, w_down, b_down, ln3_g, ln3_b):
    B, S, D = x.shape
    T = B * S
    depth = w_in.shape[0]
    tabs = _rope_tables(S)
    fconsts = _fnet_consts(S)
    lane = np.arange(LANE)
    gsum = jnp.asarray((lane[:, None] // GQA_HEAD_DIM) == (lane[None, :] // GQA_HEAD_DIM), BF16)
    mem2 = mem.reshape(B * mem.shape[1], D)
    kr_lo = w_in.shape[2] - MLA_ROPE

    h = _ln_call(x.reshape(T, D), ln_in_g, ln_in_b)
    for l in range(depth):
        zeros = lambda n: jnp.zeros((D, n), F32)
        w_in_p = jnp.concatenate([w_in[l][:, :kr_lo], zeros(MLA_NOPE), w_in[l][:, kr_lo:],
                                  zeros(LANE - MLA_NOPE - MLA_ROPE)], axis=1).astype(BF16)
        wuq = _pad_heads(w_uq[l], MLA_HEADS, MLA_NOPE + MLA_ROPE, 0, MLA_NOPE + MLA_ROPE)
        wukn = _pad_heads(w_ukv[l], MLA_HEADS, MLA_NOPE + MLA_V, 0, MLA_NOPE)
        wuv = _pad_heads(w_ukv[l], MLA_HEADS, MLA_NOPE + MLA_V, MLA_NOPE, MLA_NOPE + MLA_V)
        qg = jnp.tile(q_norm_g[l], 2).reshape(1, LANE)
        kg = jnp.tile(k_norm_g[l], 2).reshape(1, LANE)
        zf, u, qgq, kgk, vgv, qm, km, vm = _inproj_call(
            h, S, w_in_p, tabs, gsum, qg, kg, mla_q_norm_g[l].reshape(1, -1), mla_kv_norm_g[l].reshape(1, -1),
            wuq, wukn, wuv)

        yf = _fnet_call(zf, B, S, fconsts, w_f[l].astype(BF16), b_f[l])
        yc = _conv_call(u, B, S, dw_w[l], dw_b[l], conv_ln_g[l], conv_ln_b[l], w_pw[l].astype(BF16), b_pw[l])
        yg = _attn_call(qgq, kgk, vgv, B, S, (0, 0, 0, 0), (0, 0, 1, 1), "gqa_attn")
        ym = _attn_call(qm, km, vm, B, S, (0, 1, 2, 3), (0, 1, 2, 3), "mla_attn")
        h = _mixout_call((yf, yc, yg, ym), h, grp_norm_g[l], w_o[l].astype(BF16), ln1_g[l], ln1_b[l])

        kmem, vmem = _memkv_call(mem2, w_xk[l].astype(BF16), w_xv[l].astype(BF16))
        h, h3 = _xattn_call(h, B, S, w_xq[l].astype(BF16), kmem, vmem, w_xo[l].astype(BF16), ln2_g[l], ln2_b[l])

        w_r = jnp.pad(w_router[l], ((0, 0), (0, LANE - N_EXPERTS)))
        w_r_hi = w_r.astype(BF16)
        w_r_lo = (w_r - w_r_hi.astype(F32)).astype(BF16)
        b_r = jnp.concatenate([b_router[l], jnp.full((LANE - N_EXPERTS,), NEG_BIG, F32)]).reshape(1, LANE)
        idx_t, gates = _router_call(h, w_r_hi, w_r_lo, b_r)
        block_e, slot_src, slot_dst = _moe_plan(idx_t[:TOP_K], T)
        ys = _moe_call(h3, block_e, slot_src, slot_dst, w_gu[l], b_gu[l], w_down[l], b_down[l])
        h = _combine_call(ys, gates, h, ln3_g[l], ln3_b[l])
    return h.reshape(B, S, D)
```

```python
import functools
import math

import numpy as np
import jax
import jax.numpy as jnp
from jax import lax
from jax.experimental import pallas as pl
from jax.experimental.pallas import tpu as pltpu

F32 = jnp.float32
BF16 = jnp.bfloat16

D_MODEL = 1024
DEPTH = 2
GRID_W = 64
N_GROUPS = 4
GROUP_W = D_MODEL // N_GROUPS
FNET_CH = 64
CONV_K = 31
CONV_HALO = 16
GQA_HEAD_DIM = 64
GQA_Q_HEADS = 4
GQA_KV_HEADS = 2
MLA_HEADS = 4
MLA_NOPE = 64
MLA_ROPE = 32
MLA_V = 64
MLA_Q_RANK = 256
MLA_KV_RANK = 128
X_HEADS = 4
X_HEAD_DIM = D_MODEL // X_HEADS
N_EXPERTS = 32
TOP_K = 4
D_FF = D_MODEL
SWIGLU_LIMIT = 7.0
SWIGLU_ALPHA = 1.702
ROPE_THETA = 10000.0
LN_EPS = 1e-5
RMS_EPS = 1e-6
DN_ALPHA = (2 * DEPTH) ** 0.25

LANE = 128
HEAD_PAD = 128
IN_COLS_PAD = 1792
DFT_N1 = 64
MOE_BLOCK = 256
NEG_BIG = -1e30
LOG2_E = math.log2(math.e)
VMEM_LIMIT = 56 * 1024 * 1024


def _cparams(n_axes, vmem=VMEM_LIMIT):
    return pltpu.CompilerParams(dimension_semantics=("arbitrary",) * n_axes, vmem_limit_bytes=vmem)


def _full(shape):
    nd = len(shape)
    return pl.BlockSpec(shape, lambda *a: (0,) * nd)


def _dot(a, b):
    return jnp.dot(a, b, preferred_element_type=F32)


def _split(x):
    hi = x.astype(BF16)
    lo = (x - hi.astype(F32)).astype(BF16)
    return hi, lo


def _dot3(a_hi, a_lo, b_hi, b_lo):
    return _dot(a_hi, b_hi) + (_dot(a_hi, b_lo) + _dot(a_lo, b_hi))


def _layer_norm(x, g, b):
    mu = jnp.mean(x, axis=-1, keepdims=True)
    xc = x - mu
    var = jnp.mean(xc * xc, axis=-1, keepdims=True)
    return xc * lax.rsqrt(var + LN_EPS) * g + b


def _rms(x, g):
    return x * lax.rsqrt(jnp.mean(x * x, axis=-1, keepdims=True) + RMS_EPS) * g


def _rope(x, cos, sin_signed, half):
    lane = lax.broadcasted_iota(jnp.int32, x.shape, 1)
    first = (lane % (2 * half)) < half
    partner = jnp.where(first, pltpu.roll(x, LANE - half, 1), pltpu.roll(x, half, 1))
    return x * cos + partner * sin_signed


def _ln_kernel(x_ref, g_ref, b_ref, o_ref):
    o_ref[...] = _layer_norm(x_ref[...], g_ref[...], b_ref[...])


def _ln_call(x, g, b, tm=1024):
    T, D = x.shape
    return pl.pallas_call(
        _ln_kernel,
        grid=(T // tm,),
        in_specs=[pl.BlockSpec((tm, D), lambda i: (i, 0)), _full((1, D)), _full((1, D))],
        out_specs=pl.BlockSpec((tm, D), lambda i: (i, 0)),
        out_shape=jax.ShapeDtypeStruct((T, D), F32),
        compiler_params=_cparams(1),
        name="ln_in",
    )(x, g.reshape(1, D), b.reshape(1, D))


def _inproj_kernel(h_ref, w_ref, cg_ref, sg_ref, cm_ref, sm_ref, gsum_ref, qg_ref, kg_ref,
                   mqg_ref, mkg_ref, wuq_ref, wukn_ref, wuv_ref,
                   zf_ref, u_ref, qgo_ref, kgo_ref, vgo_ref, qmo_ref, kmo_ref, vmo_ref):
    z = _dot(h_ref[...].astype(BF16), w_ref[...])
    zf_ref[...] = z[:, 0:256]
    a = z[:, 256:512]
    g = z[:, 512:768]
    u_ref[...] = a * (1.0 / (1.0 + jnp.exp(-g)))

    cg = cg_ref[...]
    sg = sg_ref[...]
    cm = cm_ref[...]
    sm = sm_ref[...]
    lane = lax.broadcasted_iota(jnp.int32, cg.shape, 1)
    low = lane < GQA_HEAD_DIM
    gsum = gsum_ref[...]

    def head_norm(x, gain):
        hi, lo = _split(x * x)
        ssq = _dot(hi, gsum) + _dot(lo, gsum)
        return x * lax.rsqrt(ssq * (1.0 / GQA_HEAD_DIM) + RMS_EPS) * gain

    scale_g = GQA_HEAD_DIM ** -0.5 * LOG2_E
    q_tiles = []
    for t in range(2):
        x = z[:, 768 + t * LANE: 768 + (t + 1) * LANE]
        x = _rope(head_norm(x, qg_ref[...]), cg, sg, GQA_HEAD_DIM // 4) * scale_g
        xr = pltpu.roll(x, GQA_HEAD_DIM, 1)
        if t == 0:
            q_tiles += [jnp.where(low, x, 0.0), jnp.where(low, xr, 0.0)]
        else:
            q_tiles += [jnp.where(low, 0.0, xr), jnp.where(low, 0.0, x)]
    qgo_ref[...] = jnp.concatenate(q_tiles, axis=1).astype(BF16)

    k = z[:, 1024:1152]
    kgo_ref[...] = _rope(head_norm(k, kg_ref[...]), cg, sg, GQA_HEAD_DIM // 4).astype(BF16)

    v = z[:, 1152:1280]
    ones_col = jnp.where(lane == GQA_HEAD_DIM, 1.0, 0.0)
    v0 = jnp.where(low, v, ones_col)
    v1 = jnp.where(low, pltpu.roll(v, GQA_HEAD_DIM, 1), ones_col)
    vgo_ref[...] = jnp.concatenate([v0, v1], axis=1).astype(BF16)

    scale_m = (MLA_NOPE + MLA_ROPE) ** -0.5 * LOG2_E
    cq = _rms(z[:, 1280:1536], mqg_ref[...]).astype(BF16)
    qm = _dot(cq, wuq_ref[...])
    qm_tiles = [_rope(qm[:, t * LANE:(t + 1) * LANE], cm, sm, MLA_ROPE // 4) * scale_m
                for t in range(MLA_HEADS)]
    qmo_ref[...] = jnp.concatenate(qm_tiles, axis=1).astype(BF16)

    ckv = _rms(z[:, 1536:1664], mkg_ref[...]).astype(BF16)
    kr = _rope(z[:, 1664:1792], cm, sm, MLA_ROPE // 4)
    kn = _dot(ckv, wukn_ref[...])
    kmo_ref[...] = (kn + jnp.concatenate([kr] * MLA_HEADS, axis=1)).astype(BF16)
    vm = _dot(ckv, wuv_ref[...])
    vmo_ref[...] = (vm + jnp.concatenate([ones_col] * MLA_HEADS, axis=1)).astype(BF16)


def _inproj_call(h, S, w_in_p, tabs, gsum, qg, kg, mqg, mkg, wuq, wukn, wuv, tm=512):
    T, D = h.shape
    nsb = S // tm
    cg, sg, cm, sm = tabs
    row = lambda w: pl.BlockSpec((tm, w), lambda i: (i, 0))
    tab = pl.BlockSpec((tm, LANE), lambda i: (i % nsb, 0))
    outs = [(256, F32), (256, F32), (512, BF16), (128, BF16), (256, BF16), (512, BF16), (512, BF16), (512, BF16)]
    return pl.pallas_call(
        _inproj_kernel,
        grid=(T // tm,),
        in_specs=[row(D), _full(w_in_p.shape), tab, tab, tab, tab, _full(gsum.shape), _full(qg.shape),
                  _full(kg.shape), _full(mqg.shape), _full(mkg.shape), _full(wuq.shape), _full(wukn.shape),
                  _full(wuv.shape)],
        out_specs=[row(w) for w, _ in outs],
        out_shape=[jax.ShapeDtypeStruct((T, w), dt) for w, dt in outs],
        compiler_params=_cparams(1),
        name="in_proj",
    )(h, w_in_p, cg, sg, cm, sm, gsum, qg, kg, mqg, mkg, wuq, wukn, wuv)


def _fft1_kernel(x_ref, chi_ref, clo_ref, o_ref):
    chi = chi_ref[...]
    clo = clo_ref[...]
    for j in range(8):
        xh, xl = _split(x_ref[0, :, j, :])
        o_ref[0, :, j, :] = _dot3(chi, clo, xh, xl)


def _fft2_kernel(re_ref, im_ref, mhi_ref, mlo_ref, chh_ref, chl_ref, wf_ref, bf_ref, o_ref):
    n2 = re_ref.shape[2]
    for j in range(8):
        ab = jnp.concatenate([re_ref[0, j], im_ref[0, j]], axis=0)
        abh, abl = _split(ab)
        ri = _dot3(mhi_ref[j], mlo_ref[j], abh, abl)
        rc = jnp.concatenate([ri[:n2], ri[n2:]], axis=1)
        rch, rcl = _split(rc)
        y = _dot3(rch, rcl, chh_ref[...], chl_ref[...])
        o_ref[0, :, j, :] = _dot(y.astype(BF16), wf_ref[...]) + bf_ref[...]


def _fnet_call(zf, B, S, consts, w_f, b_f):
    C = GROUP_W
    n1, n2 = DFT_N1, S // DFT_N1
    c1h, c1l, mh, ml, chh, chl = consts
    x4 = zf.reshape(B, n1, n2, C)
    ab = pl.pallas_call(
        _fft1_kernel,
        grid=(B, n2 // 8),
        in_specs=[pl.BlockSpec((1, n1, 8, C), lambda b, i: (b, 0, i, 0)), _full(c1h.shape), _full(c1l.shape)],
        out_specs=pl.BlockSpec((1, 2 * n1, 8, C), lambda b, i: (b, 0, i, 0)),
        out_shape=jax.ShapeDtypeStruct((B, 2 * n1, n2, C), F32),
        compiler_params=_cparams(2),
        name="fnet_stage1",
    )(x4, c1h, c1l)
    g = n1 // 8
    y = pl.pallas_call(
        _fft2_kernel,
        grid=(B, g),
        in_specs=[pl.BlockSpec((1, 8, n2, C), lambda b, i: (b, i, 0, 0)),
                  pl.BlockSpec((1, 8, n2, C), lambda b, i: (b, i + g, 0, 0)),
                  pl.BlockSpec((8, 2 * n2, 2 * n2), lambda b, i: (i, 0, 0)),
                  pl.BlockSpec((8, 2 * n2, 2 * n2), lambda b, i: (i, 0, 0)),
                  _full(chh.shape), _full(chl.shape), _full(w_f.shape), _full((1, C))],
        out_specs=pl.BlockSpec((1, n2, 8, C), lambda b, i: (b, 0, i, 0)),
        out_shape=jax.ShapeDtypeStruct((B, n2, n1, C), F32),
        compiler_params=_cparams(2),
        name="fnet_stage2",
    )(ab, ab, mh, ml, chh, chl, w_f, b_f.reshape(1, C))
    return y.reshape(B * S, C)


def _fnet_consts(S):
    n1, n2 = DFT_N1, S // DFT_N1
    k1 = np.arange(n1)
    ang1 = 2.0 * np.pi * np.outer(k1, k1) / n1
    c1 = np.concatenate([np.cos(ang1), np.sin(ang1)], axis=0)
    kk = k1[:, None, None] + n1 * np.arange(n2)[None, :, None]
    ang = 2.0 * np.pi * ((kk * np.arange(n2)[None, None, :]) % S) / S
    ct, st = np.cos(ang), np.sin(ang)
    m = np.concatenate([np.concatenate([ct, -st], axis=2),
                        np.concatenate([st, ct], axis=2)], axis=1)
    kc = np.arange(FNET_CH)
    angc = 2.0 * np.pi * np.outer(kc, kc) / FNET_CH
    eye = np.eye(GROUP_W // FNET_CH)
    ch = np.concatenate([np.kron(eye, np.cos(angc)), -np.kron(eye, np.sin(angc))], axis=0)

    def split(a):
        a32 = jnp.asarray(a, F32)
        hi = a32.astype(BF16)
        return hi, (a32 - hi.astype(F32)).astype(BF16)

    return split(c1) + split(m) + split(ch)


def _conv_kernel(prev_ref, cur_ref, next_ref, w_ref, b_ref, g_ref, beta_ref, wpw_ref, bpw_ref, o_ref, ext_ref):
    i = pl.program_id(1)
    ts = cur_ref.shape[1]
    H = CONV_HALO
    ext_ref[0:H, :] = jnp.where(i > 0, prev_ref[0], 0.0)
    ext_ref[H:H + ts, :] = cur_ref[0]
    ext_ref[H + ts:H + ts + H, :] = jnp.where(i < pl.num_programs(1) - 1, next_ref[0], 0.0)
    acc = jnp.zeros((ts, cur_ref.shape[2]), F32)
    for k in range(CONV_K):
        off = H - CONV_K // 2 + k
        acc = acc + ext_ref[off:off + ts, :] * w_ref[k:k + 1, :]
    y = _layer_norm(acc + b_ref[...], g_ref[...], beta_ref[...])
    y = y * (1.0 / (1.0 + jnp.exp(-y)))
    o_ref[...] = _dot(y.astype(BF16), wpw_ref[...]) + bpw_ref[...]


def _conv_call(u, B, S, dw_w, dw_b, cg, cb, w_pw, b_pw, ts=512):
    C = GROUP_W
    H = CONV_HALO
    nb = S // ts
    r = ts // H
    u3 = u.reshape(B, S, C)
    vec = lambda a: a.reshape(1, C)
    return pl.pallas_call(
        _conv_kernel,
        grid=(B, nb),
        in_specs=[pl.BlockSpec((1, H, C), lambda b, i: (b, jnp.maximum(i * r - 1, 0), 0)),
                  pl.BlockSpec((1, ts, C), lambda b, i: (b, i, 0)),
                  pl.BlockSpec((1, H, C), lambda b, i: (b, jnp.minimum((i + 1) * r, S // H - 1), 0)),
                  _full((CONV_K, C)), _full((1, C)), _full((1, C)), _full((1, C)), _full((C, C)), _full((1, C))],
        out_specs=pl.BlockSpec((ts, C), lambda b, i: (b * nb + i, 0)),
        out_shape=jax.ShapeDtypeStruct((B * S, C), F32),
        scratch_shapes=[pltpu.VMEM((ts + 2 * H, C), F32)],
        compiler_params=_cparams(2),
        name="conv_module",
    )(u3, u3, u3, dw_w, vec(dw_b), vec(cg), vec(cb), w_pw, vec(b_pw))


def _attn_kernel(q_ref, k_ref, v_ref, o_ref, *, k_tile, v_tile, tk, n_sub):
    tq = q_ref.shape[1]
    S = k_ref.shape[1]
    dv = GQA_HEAD_DIM
    sub = tq // n_sub
    outs = []
    for h in range(len(k_tile)):
        qs = [q_ref[0, i * sub:(i + 1) * sub, h * LANE:(h + 1) * LANE] for i in range(n_sub)]
        kb, vb = k_tile[h], v_tile[h]

        def body(c, carry):
            off = pl.multiple_of(c * tk, tk)
            k = k_ref[0, pl.ds(off, tk), kb * LANE:(kb + 1) * LANE]
            v = v_ref[0, pl.ds(off, tk), vb * LANE:(vb + 1) * LANE]
            new = []
            for q, (m, acc) in zip(qs, carry):
                s = lax.dot_general(q, k, (((1,), (1,)), ((), ())), preferred_element_type=F32)
                m_new = jnp.maximum(m, jnp.max(s, axis=-1, keepdims=True))
                p = jnp.exp2(s - m_new)
                acc = jnp.exp2(m - m_new) * acc + _dot(p.astype(BF16), v)
                new.append((m_new, acc))
            return tuple(new)

        init = tuple((jnp.full((sub, 1), NEG_BIG, F32), jnp.zeros((sub, LANE), F32)) for _ in range(n_sub))
        fin = lax.fori_loop(0, S // tk, body, init)
        acc = jnp.concatenate([a for _, a in fin], axis=0)
        outs.append(acc[:, :dv] / acc[:, dv:dv + 1])
    o_ref[0] = jnp.concatenate(outs, axis=1)


def _attn_call(q, k, v, B, S, k_tile, v_tile, name, tq=1024, tk=2048, n_sub=1):
    nh = len(k_tile)
    q3 = q.reshape(B, S, q.shape[-1])
    k3 = k.reshape(B, S, k.shape[-1])
    v3 = v.reshape(B, S, v.shape[-1])
    kern = functools.partial(_attn_kernel, k_tile=k_tile, v_tile=v_tile, tk=tk, n_sub=n_sub)
    out = pl.pallas_call(
        kern,
        grid=(B, S // tq),
        in_specs=[pl.BlockSpec((1, tq, q3.shape[-1]), lambda b, i: (b, i, 0)),
                  pl.BlockSpec((1, S, k3.shape[-1]), lambda b, i: (b, 0, 0), pipeline_mode=pl.Buffered(1)),
                  pl.BlockSpec((1, S, v3.shape[-1]), lambda b, i: (b, 0, 0), pipeline_mode=pl.Buffered(1))],
        out_specs=pl.BlockSpec((1, tq, nh * GQA_HEAD_DIM), lambda b, i: (b, i, 0)),
        out_shape=jax.ShapeDtypeStruct((B, S, nh * GQA_HEAD_DIM), F32),
        compiler_params=_cparams(2),
        name=name,
    )(q3, k3, v3)
    return out.reshape(B * S, nh * GQA_HEAD_DIM)


def _mixout_kernel(yf_ref, yc_ref, yg_ref, ym_ref, h_ref, gg_ref, wo_ref, g_ref, b_ref, o_ref):
    parts = [_rms(r[...], gg_ref[i:i + 1, :]).astype(BF16)
             for i, r in enumerate((yf_ref, yc_ref, yg_ref, ym_ref))]
    mix = _dot(jnp.concatenate(parts, axis=1), wo_ref[...])
    o_ref[...] = _layer_norm(DN_ALPHA * h_ref[...] + mix, g_ref[...], b_ref[...])


def _mixout_call(ys, h, grp_g, w_o, g, b, tm=512):
    T, D = h.shape
    C = GROUP_W
    row = lambda w: pl.BlockSpec((tm, w), lambda i: (i, 0))
    return pl.pallas_call(
        _mixout_kernel,
        grid=(T // tm,),
        in_specs=[row(C)] * 4 + [row(D), _full((N_GROUPS, C)), _full((D, D)), _full((1, D)), _full((1, D))],
        out_specs=row(D),
        out_shape=jax.ShapeDtypeStruct((T, D), F32),
        compiler_params=_cparams(1),
        name="mix_out",
    )(*ys, h, grp_g, w_o, g.reshape(1, D), b.reshape(1, D))


def _memkv_kernel(m_ref, wk_ref, wv_ref, k_ref, v_ref):
    m = m_ref[...].astype(BF16)
    k_ref[...] = _dot(m, wk_ref[...]).astype(BF16)
    v_ref[...] = _dot(m, wv_ref[...]).astype(BF16)


def _memkv_call(mem2, w_xk, w_xv, tm=256):
    R, D = mem2.shape
    row = pl.BlockSpec((tm, D), lambda i: (i, 0))
    return pl.pallas_call(
        _memkv_kernel,
        grid=(R // tm,),
        in_specs=[row, _full((D, D)), _full((D, D))],
        out_specs=[row, row],
        out_shape=[jax.ShapeDtypeStruct((R, D), BF16)] * 2,
        compiler_params=_cparams(1),
        name="mem_kv",
    )(mem2, w_xk, w_xv)


def _xattn_kernel(h_ref, wq_ref, k_ref, v_ref, wo_ref, g_ref, b_ref, o_ref, o3_ref):
    h = h_ref[0]
    q = (_dot(h.astype(BF16), wq_ref[...]) * (X_HEAD_DIM ** -0.5)).astype(BF16)
    outs = []
    for hd in range(X_HEADS):
        sl = slice(hd * X_HEAD_DIM, (hd + 1) * X_HEAD_DIM)
        s = lax.dot_general(q[:, sl], k_ref[0, :, sl], (((1,), (1,)), ((), ())), preferred_element_type=F32)
        e = jnp.exp(s - jnp.max(s, axis=-1, keepdims=True))
        o = _dot(e.astype(BF16), v_ref[0, :, sl]) / jnp.sum(e, axis=-1, keepdims=True)
        outs.append(o.astype(BF16))
    xa = _dot(jnp.concatenate(outs, axis=1), wo_ref[...])
    hn = _layer_norm(DN_ALPHA * h + xa, g_ref[...], b_ref[...])
    o_ref[0] = hn
    tm = hn.shape[0]
    ns = D_MODEL // LANE
    for s in range(ns):
        o3_ref[pl.ds(s, tm, stride=ns), :] = hn[:, s * LANE:(s + 1) * LANE]


def _xattn_call(h, B, S, w_xq, kmem, vmem, w_xo, g, b, tm=256):
    D = D_MODEL
    M = kmem.shape[0] // B
    ns = D // LANE
    nsb = S // tm
    row = pl.BlockSpec((1, tm, D), lambda bb, i: (bb, i, 0))
    kv = pl.BlockSpec((1, M, D), lambda bb, i: (bb, 0, 0))
    out, out3 = pl.pallas_call(
        _xattn_kernel,
        grid=(B, nsb),
        in_specs=[row, _full((D, D)), kv, kv, _full((D, D)), _full((1, D)), _full((1, D))],
        out_specs=[row, pl.BlockSpec((tm * ns, LANE), lambda bb, i: (bb * nsb + i, 0))],
        out_shape=[jax.ShapeDtypeStruct((B, S, D), F32), jax.ShapeDtypeStruct((B * S * ns, LANE), F32)],
        compiler_params=_cparams(2),
        name="cross_attn",
    )(h.reshape(B, S, D), w_xq, kmem.reshape(B, M, D), vmem.reshape(B, M, D), w_xo, g.reshape(1, D), b.reshape(1, D))
    return out.reshape(B * S, D), out3.reshape(B * S, ns, LANE)


def _router_kernel(h_ref, whi_ref, wlo_ref, b_ref, tri_ref, idx_ref, rank_ref, gate_ref, cnt_ref, carry_ref):
    @pl.when(pl.program_id(0) == 0)
    def _():
        carry_ref[...] = jnp.zeros(carry_ref.shape, F32)

    hh, hl = _split(h_ref[...])
    logits = _dot3(hh, hl, whi_ref[...], wlo_ref[...]) + b_ref[...]
    lane = lax.broadcasted_iota(jnp.int32, logits.shape, 1)
    idx_out = jnp.zeros(logits.shape, F32)
    val_out = jnp.zeros(logits.shape, F32)
    onehot = jnp.zeros(logits.shape, F32)
    picks = []
    top = None
    denom = jnp.zeros((logits.shape[0], 1), F32)
    for k in range(TOP_K):
        mx = jnp.max(logits, axis=-1, keepdims=True)
        pick = jnp.min(jnp.where(logits == mx, lane, LANE), axis=-1, keepdims=True)
        if k == 0:
            top = mx
        e = jnp.exp(mx - top)
        denom = denom + e
        hit = lane == pick
        idx_out = jnp.where(lane == k, pick.astype(F32), idx_out)
        val_out = jnp.where(lane == k, e, val_out)
        onehot = jnp.where(hit, 1.0, onehot)
        logits = jnp.where(hit, NEG_BIG * 2.0, logits)
        picks.append(pick)
    gate_ref[...] = val_out / denom

    before = _dot(tri_ref[...], onehot.astype(BF16)) + carry_ref[0:1, :]
    rank_out = jnp.zeros(logits.shape, F32)
    for k, pick in enumerate(picks):
        r = jnp.sum(jnp.where(lane == pick, before, 0.0), axis=-1, keepdims=True)
        rank_out = jnp.where(lane == k, r, rank_out)
    total = carry_ref[0:1, :] + jnp.sum(onehot, axis=0, keepdims=True)
    carry_ref[...] = jnp.broadcast_to(total, carry_ref.shape)
    cnt_ref[...] = jnp.broadcast_to(total, cnt_ref.shape)
    idx_ref[...] = jnp.transpose(idx_out)[:8, :].astype(jnp.int32)
    rank_ref[...] = jnp.transpose(rank_out)[:8, :].astype(jnp.int32)


def _router_call(h, w_hi, w_lo, b_pad, tm=1024):
    T, D = h.shape
    tri = jnp.asarray(np.tril(np.ones((tm, tm), np.float32), -1), BF16)
    row = lambda w: pl.BlockSpec((tm, w), lambda i: (i, 0))
    col = pl.BlockSpec((8, tm), lambda i: (0, i))
    return pl.pallas_call(
        _router_kernel,
        grid=(T // tm,),
        in_specs=[row(D), _full((D, LANE)), _full((D, LANE)), _full((1, LANE)), _full((tm, tm))],
        out_specs=[col, col, row(LANE), _full((8, LANE))],
        out_shape=[jax.ShapeDtypeStruct((8, T), jnp.int32), jax.ShapeDtypeStruct((8, T), jnp.int32),
                   jax.ShapeDtypeStruct((T, LANE), F32), jax.ShapeDtypeStruct((8, LANE), F32)],
        scratch_shapes=[pltpu.VMEM((8, LANE), F32)],
        compiler_params=_cparams(1),
        name="router",
    )(h, w_hi, w_lo, b_pad, tri)


def _moe_layout(counts, idx_t, rank_t, T):
    nb = T * TOP_K // MOE_BLOCK + N_EXPERTS
    cnt = counts[0, :N_EXPERTS].astype(jnp.int32)
    nblk = (cnt + MOE_BLOCK - 1) // MOE_BLOCK
    bend = jnp.cumsum(nblk)
    bstart = bend - nblk
    blk = jnp.arange(nb, dtype=jnp.int32)
    block_e = jnp.minimum(jnp.sum((blk[:, None] >= bend[None, :]).astype(jnp.int32), axis=1), N_EXPERTS - 1)
    first_row = bstart * MOE_BLOCK
    e4 = idx_t[:TOP_K]
    base = jnp.zeros_like(e4)
    for e in range(N_EXPERTS):
        base = jnp.where(e4 == e, first_row[e], base)
    pos = (base + rank_t[:TOP_K]).astype(jnp.int32)
    meta = jnp.concatenate([jnp.where(nblk > 0, bend - 1, -1), bend[-1:]]).astype(jnp.int32)
    return block_e.astype(jnp.int32), pos, meta


DISPATCH_TOKENS = 1024
DISPATCH_UNROLL = 8


def _dispatch_kernel(meta_ref, pos_ref, h_hbm, x_hbm, zbuf, sem):
    i = pl.program_id(0)
    td = pos_ref.shape[1]
    nb = x_hbm.shape[0] // MOE_BLOCK

    def zero_block(b):
        return pltpu.make_async_copy(zbuf, x_hbm.at[pl.ds(b * MOE_BLOCK, MOE_BLOCK)], sem)

    @pl.when(i == 0)
    def _():
        zbuf[...] = jnp.zeros(zbuf.shape, F32)
        used = meta_ref[N_EXPERTS]
        for phase in ("start", "wait"):
            for e in range(N_EXPERTS):
                @pl.when(meta_ref[e] >= 0)
                def _():
                    getattr(zero_block(meta_ref[e]), phase)()

            def tail(b, _):
                getattr(zero_block(b), phase)()
                return 0

            lax.fori_loop(used, nb, tail, 0)

    base = i * td

    def body(g, _):
        t0 = g * DISPATCH_UNROLL
        rows = [[pos_ref[k, t0 + u] for k in range(TOP_K)] for u in range(DISPATCH_UNROLL)]
        for u in range(DISPATCH_UNROLL):
            for k in range(TOP_K):
                pltpu.make_async_copy(h_hbm.at[base + t0 + u], x_hbm.at[rows[u][k]], sem).start()
        return 0

    lax.fori_loop(0, td // DISPATCH_UNROLL, body, 0)
    n = td * TOP_K
    pltpu.make_async_copy(h_hbm.at[pl.ds(0, n)], x_hbm.at[pl.ds(0, n)], sem).wait()


def _dispatch_call(h3, pos, meta, nb):
    T = h3.shape[0]
    td = DISPATCH_TOKENS
    grid_spec = pltpu.PrefetchScalarGridSpec(
        num_scalar_prefetch=1,
        grid=(T // td,),
        in_specs=[pl.BlockSpec((TOP_K, td), lambda i, m: (0, i), memory_space=pltpu.SMEM),
                  pl.BlockSpec(memory_space=pl.ANY)],
        out_specs=pl.BlockSpec(memory_space=pl.ANY),
        scratch_shapes=[pltpu.VMEM((MOE_BLOCK,) + h3.shape[1:], F32), pltpu.SemaphoreType.DMA(())],
    )
    return pl.pallas_call(
        _dispatch_kernel,
        grid_spec=grid_spec,
        out_shape=jax.ShapeDtypeStruct((nb * MOE_BLOCK,) + h3.shape[1:], F32),
        compiler_params=_cparams(1),
        name="dispatch",
    )(meta, pos, h3)


MOE_CHUNKS = 4


def _moe_kernel(be_ref, meta_ref, x_ref, wgu_ref, bgu_ref, wdn_ref, bdn_ref, o_ref, wgu_bf, wdn_bf):
    j = pl.program_id(0)
    ns = D_MODEL // LANE

    @pl.when((j == 0) | (be_ref[j] != be_ref[jnp.maximum(j - 1, 0)]))
    def _():
        wgu_bf[...] = wgu_ref[0].astype(BF16)
        wdn_bf[...] = wdn_ref[0].astype(BF16)

    @pl.when(j < meta_ref[N_EXPERTS])
    def _():
        x = jnp.concatenate([x_ref[pl.ds(s, MOE_BLOCK, stride=ns), :] for s in range(ns)], axis=1).astype(BF16)
        cw = D_FF // MOE_CHUNKS
        y = None
        for c in range(MOE_CHUNKS):
            gate = _dot(x, wgu_bf[:, c * cw:(c + 1) * cw]) + bgu_ref[0, :, c * cw:(c + 1) * cw]
            up = _dot(x, wgu_bf[:, D_FF + c * cw:D_FF + (c + 1) * cw]) + bgu_ref[0, :, D_FF + c * cw:D_FF + (c + 1) * cw]
            gate = jnp.minimum(gate, SWIGLU_LIMIT)
            up = jnp.clip(up, -SWIGLU_LIMIT, SWIGLU_LIMIT)
            act = (up + 1.0) * (gate * (1.0 / (1.0 + jnp.exp(-SWIGLU_ALPHA * gate))))
            part = _dot(act.astype(BF16), wdn_bf[c * cw:(c + 1) * cw, :])
            y = part if y is None else y + part
        y = y + bdn_ref[0]
        for s in range(ns):
            o_ref[pl.ds(s, MOE_BLOCK, stride=ns), :] = y[:, s * LANE:(s + 1) * LANE]

    @pl.when(j >= meta_ref[N_EXPERTS])
    def _():
        o_ref[...] = jnp.zeros(o_ref.shape, F32)


def _moe_call(xs3, block_e, meta, w_gu, b_gu, w_down, b_down):
    D = D_MODEL
    ns = D // LANE
    nb = block_e.shape[0]
    E = w_gu.shape[0]
    rows = MOE_BLOCK * ns
    by_e = lambda shp: pl.BlockSpec((1,) + shp, lambda j, be, m: (be[j], 0, 0))
    grid_spec = pltpu.PrefetchScalarGridSpec(
        num_scalar_prefetch=2,
        grid=(nb,),
        in_specs=[pl.BlockSpec((rows, LANE), lambda j, be, m: (jnp.minimum(j, m[N_EXPERTS] - 1), 0)),
                  by_e((D, 2 * D_FF)), by_e((1, 2 * D_FF)), by_e((D_FF, D)), by_e((1, D))],
        out_specs=pl.BlockSpec((rows, LANE), lambda j, be, m: (j, 0)),
        scratch_shapes=[pltpu.VMEM((D, 2 * D_FF), BF16), pltpu.VMEM((D_FF, D), BF16)],
    )
    ys = pl.pallas_call(
        _moe_kernel,
        grid_spec=grid_spec,
        out_shape=jax.ShapeDtypeStruct((nb * rows, LANE), F32),
        compiler_params=_cparams(1),
        name="routed_experts",
    )(block_e, meta, xs3.reshape(nb * rows, LANE), w_gu, b_gu.reshape(E, 1, 2 * D_FF), w_down, b_down.reshape(E, 1, D))
    return ys.reshape(xs3.shape)


COMBINE_UNROLL = 4


def _combine_kernel(posn_ref, pos0_ref, gate_ref, h_ref, y_hbm, g_ref, b_ref, o_ref, ybuf, sems):
    i = pl.program_id(0)
    tm = h_ref.shape[0]
    ns = D_MODEL // LANE
    slot = i % 2

    def issue(pos_ref, sl):
        def body(g, _):
            r0 = g * COMBINE_UNROLL
            rows = [[pos_ref[k, r0 + u] for k in range(TOP_K)] for u in range(COMBINE_UNROLL)]
            for u in range(COMBINE_UNROLL):
                for k in range(TOP_K):
                    pltpu.make_async_copy(y_hbm.at[rows[u][k]], ybuf.at[sl, k, :, r0 + u], sems.at[sl]).start()
            return 0

        lax.fori_loop(0, tm // COMBINE_UNROLL, body, 0)

    @pl.when(i == 0)
    def _():
        issue(pos0_ref, 0)

    @pl.when(i + 1 < pl.num_programs(0))
    def _():
        issue(posn_ref, 1 - slot)

    pltpu.make_async_copy(ybuf.at[slot], ybuf.at[slot], sems.at[slot]).wait()
    gates = gate_ref[...]
    ff = jnp.zeros(h_ref.shape, F32)
    for k in range(TOP_K):
        y = jnp.concatenate([ybuf[slot, k, s] for s in range(ns)], axis=1)
        ff = ff + gates[:, k:k + 1] * y
    o_ref[...] = _layer_norm(DN_ALPHA * h_ref[...] + ff, g_ref[...], b_ref[...])


def _combine_call(ys3, pos, gates, h, g, b, tm=256):
    T, D = h.shape
    nt = T // tm
    ns = D // LANE
    row = lambda w: pl.BlockSpec((tm, w), lambda i: (i, 0))
    smem = lambda f: pl.BlockSpec((TOP_K, tm), lambda i: (0, f(i)), memory_space=pltpu.SMEM)
    return pl.pallas_call(
        _combine_kernel,
        grid=(nt,),
        in_specs=[smem(lambda i: jnp.minimum(i + 1, nt - 1)), smem(lambda i: 0), row(LANE), row(D),
                  pl.BlockSpec(memory_space=pl.ANY), _full((1, D)), _full((1, D))],
        out_specs=row(D),
        out_shape=jax.ShapeDtypeStruct((T, D), F32),
        scratch_shapes=[pltpu.VMEM((2, TOP_K, ns, tm, LANE), F32), pltpu.SemaphoreType.DMA((2,))],
        compiler_params=_cparams(1),
        name="combine",
    )(pos, pos, gates, h, ys3, g.reshape(1, D), b.reshape(1, D))


def _rope_tables(S):
    pos = np.arange(S)
    row, col = pos // GRID_W, pos % GRID_W

    def tab(p, dim):
        inv = ROPE_THETA ** (-np.arange(0, dim, 2, dtype=np.float64) / dim)
        ang = p[:, None].astype(np.float64) * inv[None, :]
        return np.cos(ang), np.sin(ang)

    def pair(dim):
        cr, sr = tab(row, dim)
        cc, sc = tab(col, dim)
        return (np.concatenate([cr, cr, cc, cc], axis=1), np.concatenate([-sr, sr, -sc, sc], axis=1))

    cg, sg = pair(GQA_HEAD_DIM // 2)
    cg, sg = np.tile(cg, (1, 2)), np.tile(sg, (1, 2))
    cm32, sm32 = pair(MLA_ROPE // 2)
    cm = np.ones((S, LANE))
    sm = np.zeros((S, LANE))
    cm[:, MLA_NOPE:MLA_NOPE + MLA_ROPE] = cm32
    sm[:, MLA_NOPE:MLA_NOPE + MLA_ROPE] = sm32
    return tuple(jnp.asarray(a, F32) for a in (cg, sg, cm, sm))


def _pad_heads(w, n_heads, width, lo, hi):
    r = w.shape[0]
    w3 = w.reshape(r, n_heads, width)[:, :, lo:hi]
    w3 = jnp.pad(w3, ((0, 0), (0, 0), (0, HEAD_PAD - (hi - lo))))
    return w3.reshape(r, n_heads * HEAD_PAD).astype(BF16)


def kernel(x, mem, ln_in_g, ln_in_b, w_in, w_f, b_f, dw_w, dw_b, conv_ln_g, conv_ln_b, w_pw, b_pw, q_norm_g, k_norm_g, mla_q_norm_g, w_uq, mla_kv_norm_g, w_ukv, grp_norm_g, w_o, ln1_g, ln1_b, w_xq, w_xk, w_xv, w_xo, ln2_g, ln2_b, w_router, b_router, w_gu, b_gu, w_down, b_down, ln3_g, ln3_b):
    B, S, D = x.shape
    T = B * S
    depth = w_in.shape[0]
    tabs = _rope_tables(S)
    fconsts = _fnet_consts(S)
    lane = np.arange(LANE)
    gsum = jnp.asarray((lane[:, None] // GQA_HEAD_DIM) == (lane[None, :] // GQA_HEAD_DIM), BF16)
    mem2 = mem.reshape(B * mem.shape[1], D)
    kr_lo = w_in.shape[2] - MLA_ROPE

    h = _ln_call(x.reshape(T, D), ln_in_g, ln_in_b)
    for l in range(depth):
        zeros = lambda n: jnp.zeros((D, n), F32)
        w_in_p = jnp.concatenate([w_in[l][:, :kr_lo], zeros(MLA_NOPE), w_in[l][:, kr_lo:],
                                  zeros(LANE - MLA_NOPE - MLA_ROPE)], axis=1).astype(BF16)
        wuq = _pad_heads(w_uq[l], MLA_HEADS, MLA_NOPE + MLA_ROPE, 0, MLA_NOPE + MLA_ROPE)
        wukn = _pad_heads(w_ukv[l], MLA_HEADS, MLA_NOPE + MLA_V, 0, MLA_NOPE)
        wuv = _pad_heads(w_ukv[l], MLA_HEADS, MLA_NOPE + MLA_V, MLA_NOPE, MLA_NOPE + MLA_V)
        qg = jnp.tile(q_norm_g[l], 2).reshape(1, LANE)
        kg = jnp.tile(k_norm_g[l], 2).reshape(1, LANE)
        zf, u, qgq, kgk, vgv, qm, km, vm = _inproj_call(
            h, S, w_in_p, tabs, gsum, qg, kg, mla_q_norm_g[l].reshape(1, -1), mla_kv_norm_g[l].reshape(1, -1),
            wuq, wukn, wuv)

        yf = _fnet_call(zf, B, S, fconsts, w_f[l].astype(BF16), b_f[l])
        yc = _conv_call(u, B, S, dw_w[l], dw_b[l], conv_ln_g[l], conv_ln_b[l], w_pw[l].astype(BF16), b_pw[l])
        yg = _attn_call(qgq, kgk, vgv, B, S, (0, 0, 0, 0), (0, 0, 1, 1), "gqa_attn")
        ym = _attn_call(qm, km, vm, B, S, (0, 1, 2, 3), (0, 1, 2, 3), "mla_attn")
        h = _mixout_call((yf, yc, yg, ym), h, grp_norm_g[l], w_o[l].astype(BF16), ln1_g[l], ln1_b[l])

        kmem, vmem = _memkv_call(mem2, w_xk[l].astype(BF16), w_xv[l].astype(BF16))
        h, h3 = _xattn_call(h, B, S, w_xq[l].astype(BF16), kmem, vmem, w_xo[l].astype(BF16), ln2_g[l], ln2_b[l])

        w_r = jnp.pad(w_router[l], ((0, 0), (0, LANE - N_EXPERTS)))
        w_r_hi = w_r.astype(BF16)
        w_r_lo = (w_r - w_r_hi.astype(F32)).astype(BF16)
        b_r = jnp.concatenate([b_router[l], jnp.full((LANE - N_EXPERTS,), NEG_BIG, F32)]).reshape(1, LANE)
        idx_t, rank_t, gates, counts = _router_call(h, w_r_hi, w_r_lo, b_r)
        block_e, pos, meta = _moe_layout(counts, idx_t, rank_t, T)
        xs3 = _dispatch_call(h3, pos, meta, block_e.shape[0])
        ys3 = _moe_call(xs3, block_e, meta, w_gu[l], b_gu[l], w_down[l], b_down[l])
        h = _combine_call(ys3, pos, gates, h, ln3_g[l], ln3_b[l])
    return h.reshape(B, S, D)
```

```python
import functools
import math

import numpy as np
import jax
import jax.numpy as jnp
from jax import lax
from jax.experimental import pallas as pl
from jax.experimental.pallas import tpu as pltpu

F32 = jnp.float32
BF16 = jnp.bfloat16

D_MODEL = 1024
DEPTH = 2
GRID_W = 64
N_GROUPS = 4
GROUP_W = D_MODEL // N_GROUPS
FNET_CH = 64
CONV_K = 31
CONV_HALO = 16
GQA_HEAD_DIM = 64
GQA_Q_HEADS = 4
GQA_KV_HEADS = 2
MLA_HEADS = 4
MLA_NOPE = 64
MLA_ROPE = 32
MLA_V = 64
MLA_Q_RANK = 256
MLA_KV_RANK = 128
X_HEADS = 4
X_HEAD_DIM = D_MODEL // X_HEADS
N_EXPERTS = 32
TOP_K = 4
D_FF = D_MODEL
SWIGLU_LIMIT = 7.0
SWIGLU_ALPHA = 1.702
ROPE_THETA = 10000.0
LN_EPS = 1e-5
RMS_EPS = 1e-6
DN_ALPHA = (2 * DEPTH) ** 0.25

LANE = 128
HEAD_PAD = 128
IN_COLS_PAD = 1792
DFT_N1 = 64
MOE_BLOCK = 256
NEG_BIG = -1e30
LOG2_E = math.log2(math.e)
VMEM_LIMIT = 56 * 1024 * 1024


def _cparams(n_axes, vmem=VMEM_LIMIT):
    return pltpu.CompilerParams(dimension_semantics=("arbitrary",) * n_axes, vmem_limit_bytes=vmem)


def _full(shape):
    nd = len(shape)
    return pl.BlockSpec(shape, lambda *a: (0,) * nd)


def _dot(a, b):
    return jnp.dot(a, b, preferred_element_type=F32)


def _split(x):
    hi = x.astype(BF16)
    lo = (x - hi.astype(F32)).astype(BF16)
    return hi, lo


def _dot3(a_hi, a_lo, b_hi, b_lo):
    return _dot(a_hi, b_hi) + (_dot(a_hi, b_lo) + _dot(a_lo, b_hi))


def _layer_norm(x, g, b):
    mu = jnp.mean(x, axis=-1, keepdims=True)
    xc = x - mu
    var = jnp.mean(xc * xc, axis=-1, keepdims=True)
    return xc * lax.rsqrt(var + LN_EPS) * g + b


def _rms(x, g):
    return x * lax.rsqrt(jnp.mean(x * x, axis=-1, keepdims=True) + RMS_EPS) * g


def _rope(x, cos, sin_signed, half):
    lane = lax.broadcasted_iota(jnp.int32, x.shape, 1)
    first = (lane % (2 * half)) < half
    partner = jnp.where(first, pltpu.roll(x, LANE - half, 1), pltpu.roll(x, half, 1))
    return x * cos + partner * sin_signed


def _ln_kernel(x_ref, g_ref, b_ref, o_ref):
    o_ref[...] = _layer_norm(x_ref[...], g_ref[...], b_ref[...])


def _ln_call(x, g, b, tm=1024):
    T, D = x.shape
    return pl.pallas_call(
        _ln_kernel,
        grid=(T // tm,),
        in_specs=[pl.BlockSpec((tm, D), lambda i: (i, 0)), _full((1, D)), _full((1, D))],
        out_specs=pl.BlockSpec((tm, D), lambda i: (i, 0)),
        out_shape=jax.ShapeDtypeStruct((T, D), F32),
        compiler_params=_cparams(1),
        name="ln_in",
    )(x, g.reshape(1, D), b.reshape(1, D))


def _inproj_kernel(h_ref, w_ref, cg_ref, sg_ref, cm_ref, sm_ref, gsum_ref, qg_ref, kg_ref,
                   mqg_ref, mkg_ref, wuq_ref, wukn_ref, wuv_ref,
                   zf_ref, u_ref, qgo_ref, kgo_ref, vgo_ref, qmo_ref, kmo_ref, vmo_ref):
    z = _dot(h_ref[...].astype(BF16), w_ref[...])
    zf_ref[...] = z[:, 0:256]
    a = z[:, 256:512]
    g = z[:, 512:768]
    u_ref[...] = a * (1.0 / (1.0 + jnp.exp(-g)))

    cg = cg_ref[...]
    sg = sg_ref[...]
    cm = cm_ref[...]
    sm = sm_ref[...]
    lane = lax.broadcasted_iota(jnp.int32, cg.shape, 1)
    low = lane < GQA_HEAD_DIM
    gsum = gsum_ref[...]

    def head_norm(x, gain):
        hi, lo = _split(x * x)
        ssq = _dot(hi, gsum) + _dot(lo, gsum)
        return x * lax.rsqrt(ssq * (1.0 / GQA_HEAD_DIM) + RMS_EPS) * gain

    scale_g = GQA_HEAD_DIM ** -0.5 * LOG2_E
    q_tiles = []
    for t in range(2):
        x = z[:, 768 + t * LANE: 768 + (t + 1) * LANE]
        x = _rope(head_norm(x, qg_ref[...]), cg, sg, GQA_HEAD_DIM // 4) * scale_g
        xr = pltpu.roll(x, GQA_HEAD_DIM, 1)
        if t == 0:
            q_tiles += [jnp.where(low, x, 0.0), jnp.where(low, xr, 0.0)]
        else:
            q_tiles += [jnp.where(low, 0.0, xr), jnp.where(low, 0.0, x)]
    qgo_ref[...] = jnp.concatenate(q_tiles, axis=1).astype(BF16)

    k = z[:, 1024:1152]
    kgo_ref[...] = _rope(head_norm(k, kg_ref[...]), cg, sg, GQA_HEAD_DIM // 4).astype(BF16)

    v = z[:, 1152:1280]
    ones_col = jnp.where(lane == GQA_HEAD_DIM, 1.0, 0.0)
    v0 = jnp.where(low, v, ones_col)
    v1 = jnp.where(low, pltpu.roll(v, GQA_HEAD_DIM, 1), ones_col)
    vgo_ref[...] = jnp.concatenate([v0, v1], axis=1).astype(BF16)

    scale_m = (MLA_NOPE + MLA_ROPE) ** -0.5 * LOG2_E
    cq = _rms(z[:, 1280:1536], mqg_ref[...]).astype(BF16)
    qm = _dot(cq, wuq_ref[...])
    qm_tiles = [_rope(qm[:, t * LANE:(t + 1) * LANE], cm, sm, MLA_ROPE // 4) * scale_m
                for t in range(MLA_HEADS)]
    qmo_ref[...] = jnp.concatenate(qm_tiles, axis=1).astype(BF16)

    ckv = _rms(z[:, 1536:1664], mkg_ref[...]).astype(BF16)
    kr = _rope(z[:, 1664:1792], cm, sm, MLA_ROPE // 4)
    kn = _dot(ckv, wukn_ref[...])
    kmo_ref[...] = (kn + jnp.concatenate([kr] * MLA_HEADS, axis=1)).astype(BF16)
    vm = _dot(ckv, wuv_ref[...])
    vmo_ref[...] = (vm + jnp.concatenate([ones_col] * MLA_HEADS, axis=1)).astype(BF16)


def _inproj_call(h, S, w_in_p, tabs, gsum, qg, kg, mqg, mkg, wuq, wukn, wuv, tm=512):
    T, D = h.shape
    nsb = S // tm
    cg, sg, cm, sm = tabs
    row = lambda w: pl.BlockSpec((tm, w), lambda i: (i, 0))
    tab = pl.BlockSpec((tm, LANE), lambda i: (i % nsb, 0))
    outs = [(256, F32), (256, F32), (512, BF16), (128, BF16), (256, BF16), (512, BF16), (512, BF16), (512, BF16)]
    return pl.pallas_call(
        _inproj_kernel,
        grid=(T // tm,),
        in_specs=[row(D), _full(w_in_p.shape), tab, tab, tab, tab, _full(gsum.shape), _full(qg.shape),
                  _full(kg.shape), _full(mqg.shape), _full(mkg.shape), _full(wuq.shape), _full(wukn.shape),
                  _full(wuv.shape)],
        out_specs=[row(w) for w, _ in outs],
        out_shape=[jax.ShapeDtypeStruct((T, w), dt) for w, dt in outs],
        compiler_params=_cparams(1),
        name="in_proj",
    )(h, w_in_p, cg, sg, cm, sm, gsum, qg, kg, mqg, mkg, wuq, wukn, wuv)


def _fft1_kernel(x_ref, chi_ref, clo_ref, o_ref):
    chi = chi_ref[...]
    clo = clo_ref[...]
    for j in range(8):
        xh, xl = _split(x_ref[0, :, j, :])
        o_ref[0, :, j, :] = _dot3(chi, clo, xh, xl)


def _fft2_kernel(re_ref, im_ref, mhi_ref, mlo_ref, chh_ref, chl_ref, wf_ref, bf_ref, o_ref):
    n2 = re_ref.shape[2]
    for j in range(8):
        ab = jnp.concatenate([re_ref[0, j], im_ref[0, j]], axis=0)
        abh, abl = _split(ab)
        ri = _dot3(mhi_ref[j], mlo_ref[j], abh, abl)
        rc = jnp.concatenate([ri[:n2], ri[n2:]], axis=1)
        rch, rcl = _split(rc)
        y = _dot3(rch, rcl, chh_ref[...], chl_ref[...])
        o_ref[0, :, j, :] = _dot(y.astype(BF16), wf_ref[...]) + bf_ref[...]


def _fnet_call(zf, B, S, consts, w_f, b_f):
    C = GROUP_W
    n1, n2 = DFT_N1, S // DFT_N1
    c1h, c1l, mh, ml, chh, chl = consts
    x4 = zf.reshape(B, n1, n2, C)
    ab = pl.pallas_call(
        _fft1_kernel,
        grid=(B, n2 // 8),
        in_specs=[pl.BlockSpec((1, n1, 8, C), lambda b, i: (b, 0, i, 0)), _full(c1h.shape), _full(c1l.shape)],
        out_specs=pl.BlockSpec((1, 2 * n1, 8, C), lambda b, i: (b, 0, i, 0)),
        out_shape=jax.ShapeDtypeStruct((B, 2 * n1, n2, C), F32),
        compiler_params=_cparams(2),
        name="fnet_stage1",
    )(x4, c1h, c1l)
    g = n1 // 8
    y = pl.pallas_call(
        _fft2_kernel,
        grid=(B, g),
        in_specs=[pl.BlockSpec((1, 8, n2, C), lambda b, i: (b, i, 0, 0)),
                  pl.BlockSpec((1, 8, n2, C), lambda b, i: (b, i + g, 0, 0)),
                  pl.BlockSpec((8, 2 * n2, 2 * n2), lambda b, i: (i, 0, 0)),
                  pl.BlockSpec((8, 2 * n2, 2 * n2), lambda b, i: (i, 0, 0)),
                  _full(chh.shape), _full(chl.shape), _full(w_f.shape), _full((1, C))],
        out_specs=pl.BlockSpec((1, n2, 8, C), lambda b, i: (b, 0, i, 0)),
        out_shape=jax.ShapeDtypeStruct((B, n2, n1, C), F32),
        compiler_params=_cparams(2),
        name="fnet_stage2",
    )(ab, ab, mh, ml, chh, chl, w_f, b_f.reshape(1, C))
    return y.reshape(B * S, C)


def _fnet_consts(S):
    n1, n2 = DFT_N1, S // DFT_N1
    k1 = np.arange(n1)
    ang1 = 2.0 * np.pi * np.outer(k1, k1) / n1
    c1 = np.concatenate([np.cos(ang1), np.sin(ang1)], axis=0)
    kk = k1[:, None, None] + n1 * np.arange(n2)[None, :, None]
    ang = 2.0 * np.pi * ((kk * np.arange(n2)[None, None, :]) % S) / S
    ct, st = np.cos(ang), np.sin(ang)
    m = np.concatenate([np.concatenate([ct, -st], axis=2),
                        np.concatenate([st, ct], axis=2)], axis=1)
    kc = np.arange(FNET_CH)
    angc = 2.0 * np.pi * np.outer(kc, kc) / FNET_CH
    eye = np.eye(GROUP_W // FNET_CH)
    ch = np.concatenate([np.kron(eye, np.cos(angc)), -np.kron(eye, np.sin(angc))], axis=0)

    def split(a):
        a32 = jnp.asarray(a, F32)
        hi = a32.astype(BF16)
        return hi, (a32 - hi.astype(F32)).astype(BF16)

    return split(c1) + split(m) + split(ch)


def _conv_kernel(prev_ref, cur_ref, next_ref, w_ref, b_ref, g_ref, beta_ref, wpw_ref, bpw_ref, o_ref, ext_ref):
    i = pl.program_id(1)
    ts = cur_ref.shape[1]
    H = CONV_HALO
    ext_ref[0:H, :] = jnp.where(i > 0, prev_ref[0], 0.0)
    ext_ref[H:H + ts, :] = cur_ref[0]
    ext_ref[H + ts:H + ts + H, :] = jnp.where(i < pl.num_programs(1) - 1, next_ref[0], 0.0)
    acc = jnp.zeros((ts, cur_ref.shape[2]), F32)
    for k in range(CONV_K):
        off = H - CONV_K // 2 + k
        acc = acc + ext_ref[off:off + ts, :] * w_ref[k:k + 1, :]
    y = _layer_norm(acc + b_ref[...], g_ref[...], beta_ref[...])
    y = y * (1.0 / (1.0 + jnp.exp(-y)))
    o_ref[...] = _dot(y.astype(BF16), wpw_ref[...]) + bpw_ref[...]


def _conv_call(u, B, S, dw_w, dw_b, cg, cb, w_pw, b_pw, ts=512):
    C = GROUP_W
    H = CONV_HALO
    nb = S // ts
    r = ts // H
    u3 = u.reshape(B, S, C)
    vec = lambda a: a.reshape(1, C)
    return pl.pallas_call(
        _conv_kernel,
        grid=(B, nb),
        in_specs=[pl.BlockSpec((1, H, C), lambda b, i: (b, jnp.maximum(i * r - 1, 0), 0)),
                  pl.BlockSpec((1, ts, C), lambda b, i: (b, i, 0)),
                  pl.BlockSpec((1, H, C), lambda b, i: (b, jnp.minimum((i + 1) * r, S // H - 1), 0)),
                  _full((CONV_K, C)), _full((1, C)), _full((1, C)), _full((1, C)), _full((C, C)), _full((1, C))],
        out_specs=pl.BlockSpec((ts, C), lambda b, i: (b * nb + i, 0)),
        out_shape=jax.ShapeDtypeStruct((B * S, C), F32),
        scratch_shapes=[pltpu.VMEM((ts + 2 * H, C), F32)],
        compiler_params=_cparams(2),
        name="conv_module",
    )(u3, u3, u3, dw_w, vec(dw_b), vec(cg), vec(cb), w_pw, vec(b_pw))


def _attn_kernel(q_ref, k_ref, v_ref, o_ref, *, k_tile, v_tile, tk, n_sub):
    tq = q_ref.shape[1]
    S = k_ref.shape[1]
    dv = GQA_HEAD_DIM
    sub = tq // n_sub
    outs = []
    for h in range(len(k_tile)):
        qs = [q_ref[0, i * sub:(i + 1) * sub, h * LANE:(h + 1) * LANE] for i in range(n_sub)]
        kb, vb = k_tile[h], v_tile[h]

        def body(c, carry):
            off = pl.multiple_of(c * tk, tk)
            k = k_ref[0, pl.ds(off, tk), kb * LANE:(kb + 1) * LANE]
            v = v_ref[0, pl.ds(off, tk), vb * LANE:(vb + 1) * LANE]
            new = []
            for q, (m, acc) in zip(qs, carry):
                s = lax.dot_general(q, k, (((1,), (1,)), ((), ())), preferred_element_type=F32)
                m_new = jnp.maximum(m, jnp.max(s, axis=-1, keepdims=True))
                p = jnp.exp2(s - m_new)
                acc = jnp.exp2(m - m_new) * acc + _dot(p.astype(BF16), v)
                new.append((m_new, acc))
            return tuple(new)

        init = tuple((jnp.full((sub, 1), NEG_BIG, F32), jnp.zeros((sub, LANE), F32)) for _ in range(n_sub))
        fin = lax.fori_loop(0, S // tk, body, init)
        acc = jnp.concatenate([a for _, a in fin], axis=0)
        outs.append(acc[:, :dv] / acc[:, dv:dv + 1])
    o_ref[0] = jnp.concatenate(outs, axis=1)


def _attn_call(q, k, v, B, S, k_tile, v_tile, name, tq=1024, tk=2048, n_sub=1):
    nh = len(k_tile)
    q3 = q.reshape(B, S, q.shape[-1])
    k3 = k.reshape(B, S, k.shape[-1])
    v3 = v.reshape(B, S, v.shape[-1])
    kern = functools.partial(_attn_kernel, k_tile=k_tile, v_tile=v_tile, tk=tk, n_sub=n_sub)
    out = pl.pallas_call(
        kern,
        grid=(B, S // tq),
        in_specs=[pl.BlockSpec((1, tq, q3.shape[-1]), lambda b, i: (b, i, 0)),
                  pl.BlockSpec((1, S, k3.shape[-1]), lambda b, i: (b, 0, 0), pipeline_mode=pl.Buffered(1)),
                  pl.BlockSpec((1, S, v3.shape[-1]), lambda b, i: (b, 0, 0), pipeline_mode=pl.Buffered(1))],
        out_specs=pl.BlockSpec((1, tq, nh * GQA_HEAD_DIM), lambda b, i: (b, i, 0)),
        out_shape=jax.ShapeDtypeStruct((B, S, nh * GQA_HEAD_DIM), F32),
        compiler_params=_cparams(2),
        name=name,
    )(q3, k3, v3)
    return out.reshape(B * S, nh * GQA_HEAD_DIM)


def _mixout_kernel(yf_ref, yc_ref, yg_ref, ym_ref, h_ref, gg_ref, wo_ref, g_ref, b_ref, o_ref):
    parts = [_rms(r[...], gg_ref[i:i + 1, :]).astype(BF16)
             for i, r in enumerate((yf_ref, yc_ref, yg_ref, ym_ref))]
    mix = _dot(jnp.concatenate(parts, axis=1), wo_ref[...])
    o_ref[...] = _layer_norm(DN_ALPHA * h_ref[...] + mix, g_ref[...], b_ref[...])


def _mixout_call(ys, h, grp_g, w_o, g, b, tm=512):
    T, D = h.shape
    C = GROUP_W
    row = lambda w: pl.BlockSpec((tm, w), lambda i: (i, 0))
    return pl.pallas_call(
        _mixout_kernel,
        grid=(T // tm,),
        in_specs=[row(C)] * 4 + [row(D), _full((N_GROUPS, C)), _full((D, D)), _full((1, D)), _full((1, D))],
        out_specs=row(D),
        out_shape=jax.ShapeDtypeStruct((T, D), F32),
        compiler_params=_cparams(1),
        name="mix_out",
    )(*ys, h, grp_g, w_o, g.reshape(1, D), b.reshape(1, D))


def _memkv_kernel(m_ref, wk_ref, wv_ref, k_ref, v_ref):
    m = m_ref[...].astype(BF16)
    k_ref[...] = _dot(m, wk_ref[...]).astype(BF16)
    v_ref[...] = _dot(m, wv_ref[...]).astype(BF16)


def _memkv_call(mem2, w_xk, w_xv, tm=256):
    R, D = mem2.shape
    row = pl.BlockSpec((tm, D), lambda i: (i, 0))
    return pl.pallas_call(
        _memkv_kernel,
        grid=(R // tm,),
        in_specs=[row, _full((D, D)), _full((D, D))],
        out_specs=[row, row],
        out_shape=[jax.ShapeDtypeStruct((R, D), BF16)] * 2,
        compiler_params=_cparams(1),
        name="mem_kv",
    )(mem2, w_xk, w_xv)


def _xattn_kernel(h_ref, wq_ref, k_ref, v_ref, wo_ref, g_ref, b_ref, o_ref, o3_ref):
    h = h_ref[0]
    q = (_dot(h.astype(BF16), wq_ref[...]) * (X_HEAD_DIM ** -0.5)).astype(BF16)
    outs = []
    for hd in range(X_HEADS):
        sl = slice(hd * X_HEAD_DIM, (hd + 1) * X_HEAD_DIM)
        s = lax.dot_general(q[:, sl], k_ref[0, :, sl], (((1,), (1,)), ((), ())), preferred_element_type=F32)
        e = jnp.exp(s - jnp.max(s, axis=-1, keepdims=True))
        o = _dot(e.astype(BF16), v_ref[0, :, sl]) / jnp.sum(e, axis=-1, keepdims=True)
        outs.append(o.astype(BF16))
    xa = _dot(jnp.concatenate(outs, axis=1), wo_ref[...])
    hn = _layer_norm(DN_ALPHA * h + xa, g_ref[...], b_ref[...])
    o_ref[0] = hn
    tm = hn.shape[0]
    ns = D_MODEL // LANE
    for s in range(ns):
        o3_ref[pl.ds(s, tm, stride=ns), :] = hn[:, s * LANE:(s + 1) * LANE]


def _xattn_call(h, B, S, w_xq, kmem, vmem, w_xo, g, b, tm=256):
    D = D_MODEL
    M = kmem.shape[0] // B
    ns = D // LANE
    nsb = S // tm
    row = pl.BlockSpec((1, tm, D), lambda bb, i: (bb, i, 0))
    kv = pl.BlockSpec((1, M, D), lambda bb, i: (bb, 0, 0))
    out, out3 = pl.pallas_call(
        _xattn_kernel,
        grid=(B, nsb),
        in_specs=[row, _full((D, D)), kv, kv, _full((D, D)), _full((1, D)), _full((1, D))],
        out_specs=[row, pl.BlockSpec((tm * ns, LANE), lambda bb, i: (bb * nsb + i, 0))],
        out_shape=[jax.ShapeDtypeStruct((B, S, D), F32), jax.ShapeDtypeStruct((B * S * ns, LANE), F32)],
        compiler_params=_cparams(2),
        name="cross_attn",
    )(h.reshape(B, S, D), w_xq, kmem.reshape(B, M, D), vmem.reshape(B, M, D), w_xo, g.reshape(1, D), b.reshape(1, D))
    return out.reshape(B * S, D), out3.reshape(B * S, ns, LANE)


def _router_kernel(h_ref, whi_ref, wlo_ref, b_ref, tri_ref, idx_ref, rank_ref, gate_ref, cnt_ref, carry_ref):
    @pl.when(pl.program_id(0) == 0)
    def _():
        carry_ref[...] = jnp.zeros(carry_ref.shape, F32)

    hh, hl = _split(h_ref[...])
    logits = _dot3(hh, hl, whi_ref[...], wlo_ref[...]) + b_ref[...]
    lane = lax.broadcasted_iota(jnp.int32, logits.shape, 1)
    idx_out = jnp.zeros(logits.shape, F32)
    val_out = jnp.zeros(logits.shape, F32)
    onehot = jnp.zeros(logits.shape, F32)
    picks = []
    top = None
    denom = jnp.zeros((logits.shape[0], 1), F32)
    for k in range(TOP_K):
        mx = jnp.max(logits, axis=-1, keepdims=True)
        pick = jnp.min(jnp.where(logits == mx, lane, LANE), axis=-1, keepdims=True)
        if k == 0:
            top = mx
        e = jnp.exp(mx - top)
        denom = denom + e
        hit = lane == pick
        idx_out = jnp.where(lane == k, pick.astype(F32), idx_out)
        val_out = jnp.where(lane == k, e, val_out)
        onehot = jnp.where(hit, 1.0, onehot)
        logits = jnp.where(hit, NEG_BIG * 2.0, logits)
        picks.append(pick)
    gate_ref[...] = val_out / denom

    before = _dot(tri_ref[...], onehot.astype(BF16)) + carry_ref[0:1, :]
    rank_out = jnp.zeros(logits.shape, F32)
    for k, pick in enumerate(picks):
        r = jnp.sum(jnp.where(lane == pick, before, 0.0), axis=-1, keepdims=True)
        rank_out = jnp.where(lane == k, r, rank_out)
    total = carry_ref[0:1, :] + jnp.sum(onehot, axis=0, keepdims=True)
    carry_ref[...] = jnp.broadcast_to(total, carry_ref.shape)
    cnt_ref[...] = jnp.broadcast_to(total, cnt_ref.shape)
    idx_ref[...] = jnp.transpose(idx_out)[:8, :].astype(jnp.int32)
    rank_ref[...] = jnp.transpose(rank_out)[:8, :].astype(jnp.int32)


def _router_call(h, w_hi, w_lo, b_pad, tm=1024):
    T, D = h.shape
    tri = jnp.asarray(np.tril(np.ones((tm, tm), np.float32), -1), BF16)
    row = lambda w: pl.BlockSpec((tm, w), lambda i: (i, 0))
    col = pl.BlockSpec((8, tm), lambda i: (0, i))
    return pl.pallas_call(
        _router_kernel,
        grid=(T // tm,),
        in_specs=[row(D), _full((D, LANE)), _full((D, LANE)), _full((1, LANE)), _full((tm, tm))],
        out_specs=[col, col, row(LANE), _full((8, LANE))],
        out_shape=[jax.ShapeDtypeStruct((8, T), jnp.int32), jax.ShapeDtypeStruct((8, T), jnp.int32),
                   jax.ShapeDtypeStruct((T, LANE), F32), jax.ShapeDtypeStruct((8, LANE), F32)],
        scratch_shapes=[pltpu.VMEM((8, LANE), F32)],
        compiler_params=_cparams(1),
        name="router",
    )(h, w_hi, w_lo, b_pad, tri)


def _moe_layout(counts, idx_t, rank_t, T):
    nb = T * TOP_K // MOE_BLOCK + N_EXPERTS
    cnt = counts[0, :N_EXPERTS].astype(jnp.int32)
    nblk = (cnt + MOE_BLOCK - 1) // MOE_BLOCK
    bend = jnp.cumsum(nblk)
    bstart = bend - nblk
    blk = jnp.arange(nb, dtype=jnp.int32)
    block_e = jnp.minimum(jnp.sum((blk[:, None] >= bend[None, :]).astype(jnp.int32), axis=1), N_EXPERTS - 1)
    first_row = bstart * MOE_BLOCK
    e4 = idx_t[:TOP_K]
    base = jnp.zeros_like(e4)
    for e in range(N_EXPERTS):
        base = jnp.where(e4 == e, first_row[e], base)
    pos = (base + rank_t[:TOP_K]).astype(jnp.int32)
    meta = jnp.concatenate([jnp.where(nblk > 0, bend - 1, -1), bend[-1:]]).astype(jnp.int32)
    return block_e.astype(jnp.int32), pos, meta


DISPATCH_TOKENS = 1024
DISPATCH_UNROLL = 8


def _dispatch_kernel(meta_ref, pos_ref, h_ref, x_hbm, zbuf, sem):
    i = pl.program_id(0)
    td = pos_ref.shape[1]
    nb = x_hbm.shape[0] // MOE_BLOCK

    def zero_block(b):
        return pltpu.make_async_copy(zbuf, x_hbm.at[pl.ds(b * MOE_BLOCK, MOE_BLOCK)], sem)

    @pl.when(i == 0)
    def _():
        zbuf[...] = jnp.zeros(zbuf.shape, F32)
        used = meta_ref[N_EXPERTS]
        for phase in ("start", "wait"):
            for e in range(N_EXPERTS):
                @pl.when(meta_ref[e] >= 0)
                def _():
                    getattr(zero_block(meta_ref[e]), phase)()

            def tail(b, _):
                getattr(zero_block(b), phase)()
                return 0

            lax.fori_loop(used, nb, tail, 0)

    def body(g, _):
        t0 = g * DISPATCH_UNROLL
        rows = [[pos_ref[k, t0 + u] for k in range(TOP_K)] for u in range(DISPATCH_UNROLL)]
        for u in range(DISPATCH_UNROLL):
            for k in range(TOP_K):
                pltpu.make_async_copy(h_ref.at[t0 + u], x_hbm.at[rows[u][k]], sem).start()
        return 0

    lax.fori_loop(0, td // DISPATCH_UNROLL, body, 0)
    for _ in range(TOP_K):
        pltpu.make_async_copy(h_ref, x_hbm.at[pl.ds(0, td)], sem).wait()


def _dispatch_call(h3, pos, meta, nb):
    T = h3.shape[0]
    td = DISPATCH_TOKENS
    grid_spec = pltpu.PrefetchScalarGridSpec(
        num_scalar_prefetch=1,
        grid=(T // td,),
        in_specs=[pl.BlockSpec((TOP_K, td), lambda i, m: (0, i), memory_space=pltpu.SMEM),
                  pl.BlockSpec((td,) + h3.shape[1:], lambda i, m: (i, 0, 0))],
        out_specs=pl.BlockSpec(memory_space=pl.ANY),
        scratch_shapes=[pltpu.VMEM((MOE_BLOCK,) + h3.shape[1:], F32), pltpu.SemaphoreType.DMA(())],
    )
    return pl.pallas_call(
        _dispatch_kernel,
        grid_spec=grid_spec,
        out_shape=jax.ShapeDtypeStruct((nb * MOE_BLOCK,) + h3.shape[1:], F32),
        compiler_params=_cparams(1),
        name="dispatch",
    )(meta, pos, h3)


MOE_CHUNKS = 4


def _moe_kernel(be_ref, meta_ref, x_ref, wgu_ref, bgu_ref, wdn_ref, bdn_ref, o_ref, wgu_bf, wdn_bf):
    j = pl.program_id(0)
    ns = D_MODEL // LANE

    @pl.when((j == 0) | (be_ref[j] != be_ref[jnp.maximum(j - 1, 0)]))
    def _():
        wgu_bf[...] = wgu_ref[0].astype(BF16)
        wdn_bf[...] = wdn_ref[0].astype(BF16)

    @pl.when(j < meta_ref[N_EXPERTS])
    def _():
        x = jnp.concatenate([x_ref[pl.ds(s, MOE_BLOCK, stride=ns), :] for s in range(ns)], axis=1).astype(BF16)
        cw = D_FF // MOE_CHUNKS
        y = None
        for c in range(MOE_CHUNKS):
            gate = _dot(x, wgu_bf[:, c * cw:(c + 1) * cw]) + bgu_ref[0, :, c * cw:(c + 1) * cw]
            up = _dot(x, wgu_bf[:, D_FF + c * cw:D_FF + (c + 1) * cw]) + bgu_ref[0, :, D_FF + c * cw:D_FF + (c + 1) * cw]
            gate = jnp.minimum(gate, SWIGLU_LIMIT)
            up = jnp.clip(up, -SWIGLU_LIMIT, SWIGLU_LIMIT)
            act = (up + 1.0) * (gate * (1.0 / (1.0 + jnp.exp(-SWIGLU_ALPHA * gate))))
            part = _dot(act.astype(BF16), wdn_bf[c * cw:(c + 1) * cw, :])
            y = part if y is None else y + part
        y = y + bdn_ref[0]
        for s in range(ns):
            o_ref[pl.ds(s, MOE_BLOCK, stride=ns), :] = y[:, s * LANE:(s + 1) * LANE]

    @pl.when(j >= meta_ref[N_EXPERTS])
    def _():
        o_ref[...] = jnp.zeros(o_ref.shape, F32)


def _moe_call(xs3, block_e, meta, w_gu, b_gu, w_down, b_down):
    D = D_MODEL
    ns = D // LANE
    nb = block_e.shape[0]
    E = w_gu.shape[0]
    rows = MOE_BLOCK * ns
    by_e = lambda shp: pl.BlockSpec((1,) + shp, lambda j, be, m: (be[j], 0, 0))
    grid_spec = pltpu.PrefetchScalarGridSpec(
        num_scalar_prefetch=2,
        grid=(nb,),
        in_specs=[pl.BlockSpec((rows, LANE), lambda j, be, m: (jnp.minimum(j, m[N_EXPERTS] - 1), 0)),
                  by_e((D, 2 * D_FF)), by_e((1, 2 * D_FF)), by_e((D_FF, D)), by_e((1, D))],
        out_specs=pl.BlockSpec((rows, LANE), lambda j, be, m: (j, 0)),
        scratch_shapes=[pltpu.VMEM((D, 2 * D_FF), BF16), pltpu.VMEM((D_FF, D), BF16)],
    )
    ys = pl.pallas_call(
        _moe_kernel,
        grid_spec=grid_spec,
        out_shape=jax.ShapeDtypeStruct((nb * rows, LANE), F32),
        compiler_params=_cparams(1),
        name="routed_experts",
    )(block_e, meta, xs3.reshape(nb * rows, LANE), w_gu, b_gu.reshape(E, 1, 2 * D_FF), w_down, b_down.reshape(E, 1, D))
    return ys.reshape(xs3.shape)


COMBINE_UNROLL = 4


def _combine_kernel(posn_ref, pos0_ref, gate_ref, h_ref, y_hbm, g_ref, b_ref, o_ref, ybuf, sems):
    i = pl.program_id(0)
    tm = h_ref.shape[0]
    ns = D_MODEL // LANE
    slot = i % 2

    def issue(pos_ref, sl):
        def body(g, _):
            r0 = g * COMBINE_UNROLL
            rows = [[pos_ref[k, r0 + u] for k in range(TOP_K)] for u in range(COMBINE_UNROLL)]
            for u in range(COMBINE_UNROLL):
                for k in range(TOP_K):
                    pltpu.make_async_copy(y_hbm.at[rows[u][k]], ybuf.at[sl, k, :, r0 + u], sems.at[sl]).start()
            return 0

        lax.fori_loop(0, tm // COMBINE_UNROLL, body, 0)

    @pl.when(i == 0)
    def _():
        issue(pos0_ref, 0)

    @pl.when(i + 1 < pl.num_programs(0))
    def _():
        issue(posn_ref, 1 - slot)

    pltpu.make_async_copy(ybuf.at[slot], ybuf.at[slot], sems.at[slot]).wait()
    gates = gate_ref[...]
    ff = jnp.zeros(h_ref.shape, F32)
    for k in range(TOP_K):
        y = jnp.concatenate([ybuf[slot, k, s] for s in range(ns)], axis=1)
        ff = ff + gates[:, k:k + 1] * y
    o_ref[...] = _layer_norm(DN_ALPHA * h_ref[...] + ff, g_ref[...], b_ref[...])


def _combine_call(ys3, pos, gates, h, g, b, tm=256):
    T, D = h.shape
    nt = T // tm
    ns = D // LANE
    row = lambda w: pl.BlockSpec((tm, w), lambda i: (i, 0))
    smem = lambda f: pl.BlockSpec((TOP_K, tm), lambda i: (0, f(i)), memory_space=pltpu.SMEM)
    return pl.pallas_call(
        _combine_kernel,
        grid=(nt,),
        in_specs=[smem(lambda i: jnp.minimum(i + 1, nt - 1)), smem(lambda i: 0), row(LANE), row(D),
                  pl.BlockSpec(memory_space=pl.ANY), _full((1, D)), _full((1, D))],
        out_specs=row(D),
        out_shape=jax.ShapeDtypeStruct((T, D), F32),
        scratch_shapes=[pltpu.VMEM((2, TOP_K, ns, tm, LANE), F32), pltpu.SemaphoreType.DMA((2,))],
        compiler_params=_cparams(1),
        name="combine",
    )(pos, pos, gates, h, ys3, g.reshape(1, D), b.reshape(1, D))


def _rope_tables(S):
    pos = np.arange(S)
    row, col = pos // GRID_W, pos % GRID_W

    def tab(p, dim):
        inv = ROPE_THETA ** (-np.arange(0, dim, 2, dtype=np.float64) / dim)
        ang = p[:, None].astype(np.float64) * inv[None, :]
        return np.cos(ang), np.sin(ang)

    def pair(dim):
        cr, sr = tab(row, dim)
        cc, sc = tab(col, dim)
        return (np.concatenate([cr, cr, cc, cc], axis=1), np.concatenate([-sr, sr, -sc, sc], axis=1))

    cg, sg = pair(GQA_HEAD_DIM // 2)
    cg, sg = np.tile(cg, (1, 2)), np.tile(sg, (1, 2))
    cm32, sm32 = pair(MLA_ROPE // 2)
    cm = np.ones((S, LANE))
    sm = np.zeros((S, LANE))
    cm[:, MLA_NOPE:MLA_NOPE + MLA_ROPE] = cm32
    sm[:, MLA_NOPE:MLA_NOPE + MLA_ROPE] = sm32
    return tuple(jnp.asarray(a, F32) for a in (cg, sg, cm, sm))


def _pad_heads(w, n_heads, width, lo, hi):
    r = w.shape[0]
    w3 = w.reshape(r, n_heads, width)[:, :, lo:hi]
    w3 = jnp.pad(w3, ((0, 0), (0, 0), (0, HEAD_PAD - (hi - lo))))
    return w3.reshape(r, n_heads * HEAD_PAD).astype(BF16)


def kernel(x, mem, ln_in_g, ln_in_b, w_in, w_f, b_f, dw_w, dw_b, conv_ln_g, conv_ln_b, w_pw, b_pw, q_norm_g, k_norm_g, mla_q_norm_g, w_uq, mla_kv_norm_g, w_ukv, grp_norm_g, w_o, ln1_g, ln1_b, w_xq, w_xk, w_xv, w_xo, ln2_g, ln2_b, w_router, b_router, w_gu, b_gu, w_down, b_down, ln3_g, ln3_b):
    B, S, D = x.shape
    T = B * S
    depth = w_in.shape[0]
    tabs = _rope_tables(S)
    fconsts = _fnet_consts(S)
    lane = np.arange(LANE)
    gsum = jnp.asarray((lane[:, None] // GQA_HEAD_DIM) == (lane[None, :] // GQA_HEAD_DIM), BF16)
    mem2 = mem.reshape(B * mem.shape[1], D)
    kr_lo = w_in.shape[2] - MLA_ROPE

    h = _ln_call(x.reshape(T, D), ln_in_g, ln_in_b)
    for l in range(depth):
        zeros = lambda n: jnp.zeros((D, n), F32)
        w_in_p = jnp.concatenate([w_in[l][:, :kr_lo], zeros(MLA_NOPE), w_in[l][:, kr_lo:],
                                  zeros(LANE - MLA_NOPE - MLA_ROPE)], axis=1).astype(BF16)
        wuq = _pad_heads(w_uq[l], MLA_HEADS, MLA_NOPE + MLA_ROPE, 0, MLA_NOPE + MLA_ROPE)
        wukn = _pad_heads(w_ukv[l], MLA_HEADS, MLA_NOPE + MLA_V, 0, MLA_NOPE)
        wuv = _pad_heads(w_ukv[l], MLA_HEADS, MLA_NOPE + MLA_V, MLA_NOPE, MLA_NOPE + MLA_V)
        qg = jnp.tile(q_norm_g[l], 2).reshape(1, LANE)
        kg = jnp.tile(k_norm_g[l], 2).reshape(1, LANE)
        zf, u, qgq, kgk, vgv, qm, km, vm = _inproj_call(
            h, S, w_in_p, tabs, gsum, qg, kg, mla_q_norm_g[l].reshape(1, -1), mla_kv_norm_g[l].reshape(1, -1),
            wuq, wukn, wuv)

        yf = _fnet_call(zf, B, S, fconsts, w_f[l].astype(BF16), b_f[l])
        yc = _conv_call(u, B, S, dw_w[l], dw_b[l], conv_ln_g[l], conv_ln_b[l], w_pw[l].astype(BF16), b_pw[l])
        yg = _attn_call(qgq, kgk, vgv, B, S, (0, 0, 0, 0), (0, 0, 1, 1), "gqa_attn")
        ym = _attn_call(qm, km, vm, B, S, (0, 1, 2, 3), (0, 1, 2, 3), "mla_attn")
        h = _mixout_call((yf, yc, yg, ym), h, grp_norm_g[l], w_o[l].astype(BF16), ln1_g[l], ln1_b[l])

        kmem, vmem = _memkv_call(mem2, w_xk[l].astype(BF16), w_xv[l].astype(BF16))
        h, h3 = _xattn_call(h, B, S, w_xq[l].astype(BF16), kmem, vmem, w_xo[l].astype(BF16), ln2_g[l], ln2_b[l])

        w_r = jnp.pad(w_router[l], ((0, 0), (0, LANE - N_EXPERTS)))
        w_r_hi = w_r.astype(BF16)
        w_r_lo = (w_r - w_r_hi.astype(F32)).astype(BF16)
        b_r = jnp.concatenate([b_router[l], jnp.full((LANE - N_EXPERTS,), NEG_BIG, F32)]).reshape(1, LANE)
        idx_t, rank_t, gates, counts = _router_call(h, w_r_hi, w_r_lo, b_r)
        block_e, pos, meta = _moe_layout(counts, idx_t, rank_t, T)
        xs3 = _dispatch_call(h3, pos, meta, block_e.shape[0])
        ys3 = _moe_call(xs3, block_e, meta, w_gu[l], b_gu[l], w_down[l], b_down[l])
        h = _combine_call(ys3, pos, gates, h, ln3_g[l], ln3_b[l])
    return h.reshape(B, S, D)
```

```python
import functools
import math

import numpy as np
import jax
import jax.numpy as jnp
from jax import lax
from jax.experimental import pallas as pl
from jax.experimental.pallas import tpu as pltpu

F32 = jnp.float32
BF16 = jnp.bfloat16

D_MODEL = 1024
DEPTH = 2
GRID_W = 64
N_GROUPS = 4
GROUP_W = D_MODEL // N_GROUPS
FNET_CH = 64
CONV_K = 31
CONV_HALO = 16
GQA_HEAD_DIM = 64
GQA_Q_HEADS = 4
GQA_KV_HEADS = 2
MLA_HEADS = 4
MLA_NOPE = 64
MLA_ROPE = 32
MLA_V = 64
MLA_Q_RANK = 256
MLA_KV_RANK = 128
X_HEADS = 4
X_HEAD_DIM = D_MODEL // X_HEADS
N_EXPERTS = 32
TOP_K = 4
D_FF = D_MODEL
SWIGLU_LIMIT = 7.0
SWIGLU_ALPHA = 1.702
ROPE_THETA = 10000.0
LN_EPS = 1e-5
RMS_EPS = 1e-6
DN_ALPHA = (2 * DEPTH) ** 0.25

LANE = 128
HEAD_PAD = 128
IN_COLS_PAD = 1792
DFT_N1 = 64
MOE_BLOCK = 256
NEG_BIG = -1e30
LOG2_E = math.log2(math.e)
VMEM_LIMIT = 56 * 1024 * 1024


def _cparams(n_axes, vmem=VMEM_LIMIT):
    return pltpu.CompilerParams(dimension_semantics=("arbitrary",) * n_axes, vmem_limit_bytes=vmem)


def _full(shape):
    nd = len(shape)
    return pl.BlockSpec(shape, lambda *a: (0,) * nd)


def _dot(a, b):
    return jnp.dot(a, b, preferred_element_type=F32)


def _split(x):
    hi = x.astype(BF16)
    lo = (x - hi.astype(F32)).astype(BF16)
    return hi, lo


def _dot3(a_hi, a_lo, b_hi, b_lo):
    return _dot(a_hi, b_hi) + (_dot(a_hi, b_lo) + _dot(a_lo, b_hi))


def _layer_norm(x, g, b):
    mu = jnp.mean(x, axis=-1, keepdims=True)
    xc = x - mu
    var = jnp.mean(xc * xc, axis=-1, keepdims=True)
    return xc * lax.rsqrt(var + LN_EPS) * g + b


def _rms(x, g):
    return x * lax.rsqrt(jnp.mean(x * x, axis=-1, keepdims=True) + RMS_EPS) * g


def _rope(x, cos, sin_signed, half):
    lane = lax.broadcasted_iota(jnp.int32, x.shape, 1)
    first = (lane % (2 * half)) < half
    partner = jnp.where(first, pltpu.roll(x, LANE - half, 1), pltpu.roll(x, half, 1))
    return x * cos + partner * sin_signed


def _ln_kernel(x_ref, g_ref, b_ref, o_ref):
    o_ref[...] = _layer_norm(x_ref[...], g_ref[...], b_ref[...])


def _ln_call(x, g, b, tm=1024):
    T, D = x.shape
    return pl.pallas_call(
        _ln_kernel,
        grid=(T // tm,),
        in_specs=[pl.BlockSpec((tm, D), lambda i: (i, 0)), _full((1, D)), _full((1, D))],
        out_specs=pl.BlockSpec((tm, D), lambda i: (i, 0)),
        out_shape=jax.ShapeDtypeStruct((T, D), F32),
        compiler_params=_cparams(1),
        name="ln_in",
    )(x, g.reshape(1, D), b.reshape(1, D))


def _inproj_kernel(h_ref, w_ref, cg_ref, sg_ref, cm_ref, sm_ref, gsum_ref, qg_ref, kg_ref,
                   mqg_ref, mkg_ref, wuq_ref, wukn_ref, wuv_ref,
                   zf_ref, u_ref, qgo_ref, kgo_ref, vgo_ref, qmo_ref, kmo_ref, vmo_ref):
    z = _dot(h_ref[...].astype(BF16), w_ref[...])
    zf_ref[...] = z[:, 0:256]
    a = z[:, 256:512]
    g = z[:, 512:768]
    u_ref[...] = a * (1.0 / (1.0 + jnp.exp(-g)))

    cg = cg_ref[...]
    sg = sg_ref[...]
    cm = cm_ref[...]
    sm = sm_ref[...]
    lane = lax.broadcasted_iota(jnp.int32, cg.shape, 1)
    low = lane < GQA_HEAD_DIM
    gsum = gsum_ref[...]

    def head_norm(x, gain):
        hi, lo = _split(x * x)
        ssq = _dot(hi, gsum) + _dot(lo, gsum)
        return x * lax.rsqrt(ssq * (1.0 / GQA_HEAD_DIM) + RMS_EPS) * gain

    scale_g = GQA_HEAD_DIM ** -0.5 * LOG2_E
    q_tiles = []
    for t in range(2):
        x = z[:, 768 + t * LANE: 768 + (t + 1) * LANE]
        x = _rope(head_norm(x, qg_ref[...]), cg, sg, GQA_HEAD_DIM // 4) * scale_g
        xr = pltpu.roll(x, GQA_HEAD_DIM, 1)
        if t == 0:
            q_tiles += [jnp.where(low, x, 0.0), jnp.where(low, xr, 0.0)]
        else:
            q_tiles += [jnp.where(low, 0.0, xr), jnp.where(low, 0.0, x)]
    qgo_ref[...] = jnp.concatenate(q_tiles, axis=1).astype(BF16)

    k = z[:, 1024:1152]
    kgo_ref[...] = _rope(head_norm(k, kg_ref[...]), cg, sg, GQA_HEAD_DIM // 4).astype(BF16)

    v = z[:, 1152:1280]
    ones_col = jnp.where(lane == GQA_HEAD_DIM, 1.0, 0.0)
    v0 = jnp.where(low, v, ones_col)
    v1 = jnp.where(low, pltpu.roll(v, GQA_HEAD_DIM, 1), ones_col)
    vgo_ref[...] = jnp.concatenate([v0, v1], axis=1).astype(BF16)

    scale_m = (MLA_NOPE + MLA_ROPE) ** -0.5 * LOG2_E
    cq = _rms(z[:, 1280:1536], mqg_ref[...]).astype(BF16)
    qm = _dot(cq, wuq_ref[...])
    qm_tiles = [_rope(qm[:, t * LANE:(t + 1) * LANE], cm, sm, MLA_ROPE // 4) * scale_m
                for t in range(MLA_HEADS)]
    qmo_ref[...] = jnp.concatenate(qm_tiles, axis=1).astype(BF16)

    ckv = _rms(z[:, 1536:1664], mkg_ref[...]).astype(BF16)
    kr = _rope(z[:, 1664:1792], cm, sm, MLA_ROPE // 4)
    kn = _dot(ckv, wukn_ref[...])
    kmo_ref[...] = (kn + jnp.concatenate([kr] * MLA_HEADS, axis=1)).astype(BF16)
    vm = _dot(ckv, wuv_ref[...])
    vmo_ref[...] = (vm + jnp.concatenate([ones_col] * MLA_HEADS, axis=1)).astype(BF16)


def _inproj_call(h, S, w_in_p, tabs, gsum, qg, kg, mqg, mkg, wuq, wukn, wuv, tm=512):
    T, D = h.shape
    nsb = S // tm
    cg, sg, cm, sm = tabs
    row = lambda w: pl.BlockSpec((tm, w), lambda i: (i, 0))
    tab = pl.BlockSpec((tm, LANE), lambda i: (i % nsb, 0))
    outs = [(256, F32), (256, F32), (512, BF16), (128, BF16), (256, BF16), (512, BF16), (512, BF16), (512, BF16)]
    return pl.pallas_call(
        _inproj_kernel,
        grid=(T // tm,),
        in_specs=[row(D), _full(w_in_p.shape), tab, tab, tab, tab, _full(gsum.shape), _full(qg.shape),
                  _full(kg.shape), _full(mqg.shape), _full(mkg.shape), _full(wuq.shape), _full(wukn.shape),
                  _full(wuv.shape)],
        out_specs=[row(w) for w, _ in outs],
        out_shape=[jax.ShapeDtypeStruct((T, w), dt) for w, dt in outs],
        compiler_params=_cparams(1),
        name="in_proj",
    )(h, w_in_p, cg, sg, cm, sm, gsum, qg, kg, mqg, mkg, wuq, wukn, wuv)


def _fft1_kernel(x_ref, chi_ref, clo_ref, o_ref):
    chi = chi_ref[...]
    clo = clo_ref[...]
    for j in range(8):
        xh, xl = _split(x_ref[0, :, j, :])
        o_ref[0, :, j, :] = _dot3(chi, clo, xh, xl)


def _fft2_kernel(re_ref, im_ref, mhi_ref, mlo_ref, chh_ref, chl_ref, wf_ref, bf_ref, o_ref):
    n2 = re_ref.shape[2]
    for j in range(8):
        ab = jnp.concatenate([re_ref[0, j], im_ref[0, j]], axis=0)
        abh, abl = _split(ab)
        ri = _dot3(mhi_ref[j], mlo_ref[j], abh, abl)
        rc = jnp.concatenate([ri[:n2], ri[n2:]], axis=1)
        rch, rcl = _split(rc)
        y = _dot3(rch, rcl, chh_ref[...], chl_ref[...])
        o_ref[0, :, j, :] = _dot(y.astype(BF16), wf_ref[...]) + bf_ref[...]


def _fnet_call(zf, B, S, consts, w_f, b_f):
    C = GROUP_W
    n1, n2 = DFT_N1, S // DFT_N1
    c1h, c1l, mh, ml, chh, chl = consts
    x4 = zf.reshape(B, n1, n2, C)
    ab = pl.pallas_call(
        _fft1_kernel,
        grid=(B, n2 // 8),
        in_specs=[pl.BlockSpec((1, n1, 8, C), lambda b, i: (b, 0, i, 0)), _full(c1h.shape), _full(c1l.shape)],
        out_specs=pl.BlockSpec((1, 2 * n1, 8, C), lambda b, i: (b, 0, i, 0)),
        out_shape=jax.ShapeDtypeStruct((B, 2 * n1, n2, C), F32),
        compiler_params=_cparams(2),
        name="fnet_stage1",
    )(x4, c1h, c1l)
    g = n1 // 8
    y = pl.pallas_call(
        _fft2_kernel,
        grid=(B, g),
        in_specs=[pl.BlockSpec((1, 8, n2, C), lambda b, i: (b, i, 0, 0)),
                  pl.BlockSpec((1, 8, n2, C), lambda b, i: (b, i + g, 0, 0)),
                  pl.BlockSpec((8, 2 * n2, 2 * n2), lambda b, i: (i, 0, 0)),
                  pl.BlockSpec((8, 2 * n2, 2 * n2), lambda b, i: (i, 0, 0)),
                  _full(chh.shape), _full(chl.shape), _full(w_f.shape), _full((1, C))],
        out_specs=pl.BlockSpec((1, n2, 8, C), lambda b, i: (b, 0, i, 0)),
        out_shape=jax.ShapeDtypeStruct((B, n2, n1, C), F32),
        compiler_params=_cparams(2),
        name="fnet_stage2",
    )(ab, ab, mh, ml, chh, chl, w_f, b_f.reshape(1, C))
    return y.reshape(B * S, C)


def _fnet_consts(S):
    n1, n2 = DFT_N1, S // DFT_N1
    k1 = np.arange(n1)
    ang1 = 2.0 * np.pi * np.outer(k1, k1) / n1
    c1 = np.concatenate([np.cos(ang1), np.sin(ang1)], axis=0)
    kk = k1[:, None, None] + n1 * np.arange(n2)[None, :, None]
    ang = 2.0 * np.pi * ((kk * np.arange(n2)[None, None, :]) % S) / S
    ct, st = np.cos(ang), np.sin(ang)
    m = np.concatenate([np.concatenate([ct, -st], axis=2),
                        np.concatenate([st, ct], axis=2)], axis=1)
    kc = np.arange(FNET_CH)
    angc = 2.0 * np.pi * np.outer(kc, kc) / FNET_CH
    eye = np.eye(GROUP_W // FNET_CH)
    ch = np.concatenate([np.kron(eye, np.cos(angc)), -np.kron(eye, np.sin(angc))], axis=0)

    def split(a):
        a32 = jnp.asarray(a, F32)
        hi = a32.astype(BF16)
        return hi, (a32 - hi.astype(F32)).astype(BF16)

    return split(c1) + split(m) + split(ch)


def _conv_kernel(prev_ref, cur_ref, next_ref, w_ref, b_ref, g_ref, beta_ref, wpw_ref, bpw_ref, o_ref, ext_ref):
    i = pl.program_id(1)
    ts = cur_ref.shape[1]
    H = CONV_HALO
    ext_ref[0:H, :] = jnp.where(i > 0, prev_ref[0], 0.0)
    ext_ref[H:H + ts, :] = cur_ref[0]
    ext_ref[H + ts:H + ts + H, :] = jnp.where(i < pl.num_programs(1) - 1, next_ref[0], 0.0)
    acc = jnp.zeros((ts, cur_ref.shape[2]), F32)
    for k in range(CONV_K):
        off = H - CONV_K // 2 + k
        acc = acc + ext_ref[off:off + ts, :] * w_ref[k:k + 1, :]
    y = _layer_norm(acc + b_ref[...], g_ref[...], beta_ref[...])
    y = y * (1.0 / (1.0 + jnp.exp(-y)))
    o_ref[...] = _dot(y.astype(BF16), wpw_ref[...]) + bpw_ref[...]


def _conv_call(u, B, S, dw_w, dw_b, cg, cb, w_pw, b_pw, ts=512):
    C = GROUP_W
    H = CONV_HALO
    nb = S // ts
    r = ts // H
    u3 = u.reshape(B, S, C)
    vec = lambda a: a.reshape(1, C)
    return pl.pallas_call(
        _conv_kernel,
        grid=(B, nb),
        in_specs=[pl.BlockSpec((1, H, C), lambda b, i: (b, jnp.maximum(i * r - 1, 0), 0)),
                  pl.BlockSpec((1, ts, C), lambda b, i: (b, i, 0)),
                  pl.BlockSpec((1, H, C), lambda b, i: (b, jnp.minimum((i + 1) * r, S // H - 1), 0)),
                  _full((CONV_K, C)), _full((1, C)), _full((1, C)), _full((1, C)), _full((C, C)), _full((1, C))],
        out_specs=pl.BlockSpec((ts, C), lambda b, i: (b * nb + i, 0)),
        out_shape=jax.ShapeDtypeStruct((B * S, C), F32),
        scratch_shapes=[pltpu.VMEM((ts + 2 * H, C), F32)],
        compiler_params=_cparams(2),
        name="conv_module",
    )(u3, u3, u3, dw_w, vec(dw_b), vec(cg), vec(cb), w_pw, vec(b_pw))


def _attn_kernel(q_ref, k_ref, v_ref, o_ref, *, k_tile, v_tile, tk, n_sub):
    tq = q_ref.shape[1]
    S = k_ref.shape[1]
    dv = GQA_HEAD_DIM
    sub = tq // n_sub
    outs = []
    for h in range(len(k_tile)):
        qs = [q_ref[0, i * sub:(i + 1) * sub, h * LANE:(h + 1) * LANE] for i in range(n_sub)]
        kb, vb = k_tile[h], v_tile[h]

        def body(c, carry):
            off = pl.multiple_of(c * tk, tk)
            k = k_ref[0, pl.ds(off, tk), kb * LANE:(kb + 1) * LANE]
            v = v_ref[0, pl.ds(off, tk), vb * LANE:(vb + 1) * LANE]
            new = []
            for q, (m, acc) in zip(qs, carry):
                s = lax.dot_general(q, k, (((1,), (1,)), ((), ())), preferred_element_type=F32)
                m_new = jnp.maximum(m, jnp.max(s, axis=-1, keepdims=True))
                p = jnp.exp2(s - m_new)
                acc = jnp.exp2(m - m_new) * acc + _dot(p.astype(BF16), v)
                new.append((m_new, acc))
            return tuple(new)

        init = tuple((jnp.full((sub, 1), NEG_BIG, F32), jnp.zeros((sub, LANE), F32)) for _ in range(n_sub))
        fin = lax.fori_loop(0, S // tk, body, init)
        acc = jnp.concatenate([a for _, a in fin], axis=0)
        outs.append(acc[:, :dv] / acc[:, dv:dv + 1])
    o_ref[0] = jnp.concatenate(outs, axis=1)


def _attn_call(q, k, v, B, S, k_tile, v_tile, name, tq=1024, tk=2048, n_sub=1):
    nh = len(k_tile)
    q3 = q.reshape(B, S, q.shape[-1])
    k3 = k.reshape(B, S, k.shape[-1])
    v3 = v.reshape(B, S, v.shape[-1])
    kern = functools.partial(_attn_kernel, k_tile=k_tile, v_tile=v_tile, tk=tk, n_sub=n_sub)
    out = pl.pallas_call(
        kern,
        grid=(B, S // tq),
        in_specs=[pl.BlockSpec((1, tq, q3.shape[-1]), lambda b, i: (b, i, 0)),
                  pl.BlockSpec((1, S, k3.shape[-1]), lambda b, i: (b, 0, 0), pipeline_mode=pl.Buffered(1)),
                  pl.BlockSpec((1, S, v3.shape[-1]), lambda b, i: (b, 0, 0), pipeline_mode=pl.Buffered(1))],
        out_specs=pl.BlockSpec((1, tq, nh * GQA_HEAD_DIM), lambda b, i: (b, i, 0)),
        out_shape=jax.ShapeDtypeStruct((B, S, nh * GQA_HEAD_DIM), F32),
        compiler_params=_cparams(2),
        name=name,
    )(q3, k3, v3)
    return out.reshape(B * S, nh * GQA_HEAD_DIM)


def _mixout_kernel(yf_ref, yc_ref, yg_ref, ym_ref, h_ref, gg_ref, wo_ref, g_ref, b_ref, o_ref):
    parts = [_rms(r[...], gg_ref[i:i + 1, :]).astype(BF16)
             for i, r in enumerate((yf_ref, yc_ref, yg_ref, ym_ref))]
    mix = _dot(jnp.concatenate(parts, axis=1), wo_ref[...])
    o_ref[...] = _layer_norm(DN_ALPHA * h_ref[...] + mix, g_ref[...], b_ref[...])


def _mixout_call(ys, h, grp_g, w_o, g, b, tm=512):
    T, D = h.shape
    C = GROUP_W
    row = lambda w: pl.BlockSpec((tm, w), lambda i: (i, 0))
    return pl.pallas_call(
        _mixout_kernel,
        grid=(T // tm,),
        in_specs=[row(C)] * 4 + [row(D), _full((N_GROUPS, C)), _full((D, D)), _full((1, D)), _full((1, D))],
        out_specs=row(D),
        out_shape=jax.ShapeDtypeStruct((T, D), F32),
        compiler_params=_cparams(1),
        name="mix_out",
    )(*ys, h, grp_g, w_o, g.reshape(1, D), b.reshape(1, D))


def _memkv_kernel(m_ref, wk_ref, wv_ref, k_ref, v_ref):
    m = m_ref[...].astype(BF16)
    k_ref[...] = _dot(m, wk_ref[...]).astype(BF16)
    v_ref[...] = _dot(m, wv_ref[...]).astype(BF16)


def _memkv_call(mem2, w_xk, w_xv, tm=256):
    R, D = mem2.shape
    row = pl.BlockSpec((tm, D), lambda i: (i, 0))
    return pl.pallas_call(
        _memkv_kernel,
        grid=(R // tm,),
        in_specs=[row, _full((D, D)), _full((D, D))],
        out_specs=[row, row],
        out_shape=[jax.ShapeDtypeStruct((R, D), BF16)] * 2,
        compiler_params=_cparams(1),
        name="mem_kv",
    )(mem2, w_xk, w_xv)


def _xattn_kernel(h_ref, wq_ref, k_ref, v_ref, wo_ref, g_ref, b_ref, o_ref, o3_ref):
    h = h_ref[0]
    q = (_dot(h.astype(BF16), wq_ref[...]) * (X_HEAD_DIM ** -0.5)).astype(BF16)
    outs = []
    for hd in range(X_HEADS):
        sl = slice(hd * X_HEAD_DIM, (hd + 1) * X_HEAD_DIM)
        s = lax.dot_general(q[:, sl], k_ref[0, :, sl], (((1,), (1,)), ((), ())), preferred_element_type=F32)
        e = jnp.exp(s - jnp.max(s, axis=-1, keepdims=True))
        o = _dot(e.astype(BF16), v_ref[0, :, sl]) / jnp.sum(e, axis=-1, keepdims=True)
        outs.append(o.astype(BF16))
    xa = _dot(jnp.concatenate(outs, axis=1), wo_ref[...])
    hn = _layer_norm(DN_ALPHA * h + xa, g_ref[...], b_ref[...])
    o_ref[0] = hn
    tm = hn.shape[0]
    ns = D_MODEL // LANE
    for s in range(ns):
        o3_ref[pl.ds(s, tm, stride=ns), :] = hn[:, s * LANE:(s + 1) * LANE]


def _xattn_call(h, B, S, w_xq, kmem, vmem, w_xo, g, b, tm=256):
    D = D_MODEL
    M = kmem.shape[0] // B
    ns = D // LANE
    nsb = S // tm
    row = pl.BlockSpec((1, tm, D), lambda bb, i: (bb, i, 0))
    kv = pl.BlockSpec((1, M, D), lambda bb, i: (bb, 0, 0))
    out, out3 = pl.pallas_call(
        _xattn_kernel,
        grid=(B, nsb),
        in_specs=[row, _full((D, D)), kv, kv, _full((D, D)), _full((1, D)), _full((1, D))],
        out_specs=[row, pl.BlockSpec((tm * ns, LANE), lambda bb, i: (bb * nsb + i, 0))],
        out_shape=[jax.ShapeDtypeStruct((B, S, D), F32), jax.ShapeDtypeStruct((B * S * ns, LANE), F32)],
        compiler_params=_cparams(2),
        name="cross_attn",
    )(h.reshape(B, S, D), w_xq, kmem.reshape(B, M, D), vmem.reshape(B, M, D), w_xo, g.reshape(1, D), b.reshape(1, D))
    return out.reshape(B * S, D), out3.reshape(B * S, ns, LANE)


def _router_kernel(h_ref, whi_ref, wlo_ref, b_ref, tri_ref, idx_ref, rank_ref, gate_ref, cnt_ref, carry_ref):
    @pl.when(pl.program_id(0) == 0)
    def _():
        carry_ref[...] = jnp.zeros(carry_ref.shape, F32)

    hh, hl = _split(h_ref[...])
    logits = _dot3(hh, hl, whi_ref[...], wlo_ref[...]) + b_ref[...]
    lane = lax.broadcasted_iota(jnp.int32, logits.shape, 1)
    idx_out = jnp.zeros(logits.shape, F32)
    val_out = jnp.zeros(logits.shape, F32)
    onehot = jnp.zeros(logits.shape, F32)
    picks = []
    top = None
    denom = jnp.zeros((logits.shape[0], 1), F32)
    for k in range(TOP_K):
        mx = jnp.max(logits, axis=-1, keepdims=True)
        pick = jnp.min(jnp.where(logits == mx, lane, LANE), axis=-1, keepdims=True)
        if k == 0:
            top = mx
        e = jnp.exp(mx - top)
        denom = denom + e
        hit = lane == pick
        idx_out = jnp.where(lane == k, pick.astype(F32), idx_out)
        val_out = jnp.where(lane == k, e, val_out)
        onehot = jnp.where(hit, 1.0, onehot)
        logits = jnp.where(hit, NEG_BIG * 2.0, logits)
        picks.append(pick)
    gate_ref[...] = val_out / denom

    before = _dot(tri_ref[...], onehot.astype(BF16)) + carry_ref[0:1, :]
    rank_out = jnp.zeros(logits.shape, F32)
    for k, pick in enumerate(picks):
        r = jnp.sum(jnp.where(lane == pick, before, 0.0), axis=-1, keepdims=True)
        rank_out = jnp.where(lane == k, r, rank_out)
    total = carry_ref[0:1, :] + jnp.sum(onehot, axis=0, keepdims=True)
    carry_ref[...] = jnp.broadcast_to(total, carry_ref.shape)
    cnt_ref[...] = jnp.broadcast_to(total, cnt_ref.shape)
    idx_ref[...] = jnp.transpose(idx_out)[:8, :].astype(jnp.int32)
    rank_ref[...] = jnp.transpose(rank_out)[:8, :].astype(jnp.int32)


def _router_call(h, w_hi, w_lo, b_pad, tm=1024):
    T, D = h.shape
    tri = jnp.asarray(np.tril(np.ones((tm, tm), np.float32), -1), BF16)
    row = lambda w: pl.BlockSpec((tm, w), lambda i: (i, 0))
    col = pl.BlockSpec((8, tm), lambda i: (0, i))
    return pl.pallas_call(
        _router_kernel,
        grid=(T // tm,),
        in_specs=[row(D), _full((D, LANE)), _full((D, LANE)), _full((1, LANE)), _full((tm, tm))],
        out_specs=[col, col, row(LANE), _full((8, LANE))],
        out_shape=[jax.ShapeDtypeStruct((8, T), jnp.int32), jax.ShapeDtypeStruct((8, T), jnp.int32),
                   jax.ShapeDtypeStruct((T, LANE), F32), jax.ShapeDtypeStruct((8, LANE), F32)],
        scratch_shapes=[pltpu.VMEM((8, LANE), F32)],
        compiler_params=_cparams(1),
        name="router",
    )(h, w_hi, w_lo, b_pad, tri)


def _moe_layout(counts, idx_t, rank_t, T):
    nb = T * TOP_K // MOE_BLOCK + N_EXPERTS
    cnt = counts[0, :N_EXPERTS].astype(jnp.int32)
    nblk = (cnt + MOE_BLOCK - 1) // MOE_BLOCK
    bend = jnp.cumsum(nblk)
    bstart = bend - nblk
    blk = jnp.arange(nb, dtype=jnp.int32)
    block_e = jnp.minimum(jnp.sum((blk[:, None] >= bend[None, :]).astype(jnp.int32), axis=1), N_EXPERTS - 1)
    first_row = bstart * MOE_BLOCK
    e4 = idx_t[:TOP_K]
    base = jnp.zeros_like(e4)
    for e in range(N_EXPERTS):
        base = jnp.where(e4 == e, first_row[e], base)
    pos = (base + rank_t[:TOP_K]).astype(jnp.int32)
    meta = jnp.concatenate([jnp.where(nblk > 0, bend - 1, -1), bend[-1:]]).astype(jnp.int32)
    return block_e.astype(jnp.int32), pos, meta


DISPATCH_TOKENS = 1024
DISPATCH_UNROLL = 8


def _dispatch_kernel(meta_ref, pos_ref, h_ref, x_hbm, zbuf, sem):
    i = pl.program_id(0)
    td = h_ref.shape[0]
    nb = x_hbm.shape[0] // MOE_BLOCK

    def zero_block(b):
        return pltpu.make_async_copy(zbuf, x_hbm.at[pl.ds(b * MOE_BLOCK, MOE_BLOCK)], sem)

    @pl.when(i == 0)
    def _():
        zbuf[...] = jnp.zeros(zbuf.shape, F32)
        used = meta_ref[N_EXPERTS]
        for phase in ("start", "wait"):
            for e in range(N_EXPERTS):
                @pl.when(meta_ref[e] >= 0)
                def _():
                    getattr(zero_block(meta_ref[e]), phase)()

            def tail(b, _):
                getattr(zero_block(b), phase)()
                return 0

            lax.fori_loop(used, nb, tail, 0)

    def body(g, _):
        t0 = g * DISPATCH_UNROLL
        rows = [[pos_ref[0, 0, k * td + t0 + u] for k in range(TOP_K)] for u in range(DISPATCH_UNROLL)]
        for u in range(DISPATCH_UNROLL):
            for k in range(TOP_K):
                pltpu.make_async_copy(h_ref.at[t0 + u], x_hbm.at[rows[u][k]], sem).start()
        return 0

    lax.fori_loop(0, td // DISPATCH_UNROLL, body, 0)
    for _ in range(TOP_K):
        pltpu.make_async_copy(h_ref, x_hbm.at[pl.ds(0, td)], sem).wait()


def _dispatch_call(h3, pos, meta, nb):
    T = h3.shape[0]
    td = DISPATCH_TOKENS
    grid_spec = pltpu.PrefetchScalarGridSpec(
        num_scalar_prefetch=1,
        grid=(T // td,),
        in_specs=[pl.BlockSpec((1, 1, TOP_K * td), lambda i, m: (i, 0, 0), memory_space=pltpu.SMEM),
                  pl.BlockSpec((td,) + h3.shape[1:], lambda i, m: (i, 0, 0))],
        out_specs=pl.BlockSpec(memory_space=pl.ANY),
        scratch_shapes=[pltpu.VMEM((MOE_BLOCK,) + h3.shape[1:], F32), pltpu.SemaphoreType.DMA(())],
    )
    return pl.pallas_call(
        _dispatch_kernel,
        grid_spec=grid_spec,
        out_shape=jax.ShapeDtypeStruct((nb * MOE_BLOCK,) + h3.shape[1:], F32),
        compiler_params=_cparams(1),
        name="dispatch",
    )(meta, _tile_major(pos, td), h3)


MOE_CHUNKS = 4


def _moe_kernel(be_ref, meta_ref, x_ref, wgu_ref, bgu_ref, wdn_ref, bdn_ref, o_ref, wgu_bf, wdn_bf, act_ref):
    j = pl.program_id(0)
    ns = D_MODEL // LANE

    @pl.when((j == 0) | (be_ref[j] != be_ref[jnp.maximum(j - 1, 0)]))
    def _():
        wgu_bf[...] = wgu_ref[0, 0].astype(BF16)
        wdn_bf[...] = wdn_ref[0, 0].astype(BF16)

    @pl.when(j < meta_ref[N_EXPERTS])
    def _():
        x = jnp.concatenate([x_ref[pl.ds(s, MOE_BLOCK, stride=ns), :] for s in range(ns)], axis=1).astype(BF16)
        cw = D_FF // MOE_CHUNKS
        for c in range(MOE_CHUNKS):
            gate = _dot(x, wgu_bf[:, c * cw:(c + 1) * cw]) + bgu_ref[0, 0, :, c * cw:(c + 1) * cw]
            up = _dot(x, wgu_bf[:, D_FF + c * cw:D_FF + (c + 1) * cw]) + bgu_ref[0, 0, :, D_FF + c * cw:D_FF + (c + 1) * cw]
            gate = jnp.minimum(gate, SWIGLU_LIMIT)
            up = jnp.clip(up, -SWIGLU_LIMIT, SWIGLU_LIMIT)
            act = (up + 1.0) * (gate * (1.0 / (1.0 + jnp.exp(-SWIGLU_ALPHA * gate))))
            act_ref[:, c * cw:(c + 1) * cw] = act.astype(BF16)
        ow = D_MODEL // MOE_CHUNKS
        for n in range(MOE_CHUNKS):
            yn = _dot(act_ref[...], wdn_bf[:, n * ow:(n + 1) * ow]) + bdn_ref[0, 0, :, n * ow:(n + 1) * ow]
            for t in range(ow // LANE):
                s = n * (ow // LANE) + t
                o_ref[pl.ds(s, MOE_BLOCK, stride=ns), :] = yn[:, t * LANE:(t + 1) * LANE]

    @pl.when(j >= meta_ref[N_EXPERTS])
    def _():
        o_ref[...] = jnp.zeros(o_ref.shape, F32)


def _moe_call(xs3, block_e, meta, layer, w_gu, b_gu, w_down, b_down):
    D = D_MODEL
    ns = D // LANE
    nb = block_e.shape[0]
    L, E = w_gu.shape[:2]
    rows = MOE_BLOCK * ns
    by_e = lambda shp: pl.BlockSpec((1, 1) + shp, lambda j, be, m: (layer, be[j], 0, 0))
    grid_spec = pltpu.PrefetchScalarGridSpec(
        num_scalar_prefetch=2,
        grid=(nb,),
        in_specs=[pl.BlockSpec((rows, LANE), lambda j, be, m: (jnp.minimum(j, m[N_EXPERTS] - 1), 0)),
                  by_e((D, 2 * D_FF)), by_e((1, 2 * D_FF)), by_e((D_FF, D)), by_e((1, D))],
        out_specs=pl.BlockSpec((rows, LANE), lambda j, be, m: (j, 0)),
        scratch_shapes=[pltpu.VMEM((D, 2 * D_FF), BF16), pltpu.VMEM((D_FF, D), BF16),
                        pltpu.VMEM((MOE_BLOCK, D_FF), BF16)],
    )
    ys = pl.pallas_call(
        _moe_kernel,
        grid_spec=grid_spec,
        out_shape=jax.ShapeDtypeStruct((nb * rows, LANE), F32),
        compiler_params=_cparams(1),
        name="routed_experts",
    )(block_e, meta, xs3.reshape(nb * rows, LANE), w_gu, b_gu.reshape(L, E, 1, 2 * D_FF), w_down,
      b_down.reshape(L, E, 1, D))
    return ys.reshape(xs3.shape)


COMBINE_UNROLL = 8


def _tile_major(pos, tile):
    T = pos.shape[1]
    return pos.reshape(TOP_K, T // tile, tile).transpose(1, 0, 2).reshape(T // tile, 1, TOP_K * tile)


def _combine_kernel(posn_ref, pos0_ref, gate_ref, h_ref, y_hbm, g_ref, b_ref, o_ref, ybuf, sems):
    i = pl.program_id(0)
    tm = h_ref.shape[0]
    ns = D_MODEL // LANE
    slot = i % 2

    def copy(row, sl, k, r):
        return pltpu.make_async_copy(y_hbm.at[row], ybuf.at[sl, k, :, r], sems.at[sl])

    def issue(pos_ref, sl):
        def body(g, _):
            r0 = g * COMBINE_UNROLL
            rows = [[pos_ref[0, 0, k * tm + r0 + u] for k in range(TOP_K)] for u in range(COMBINE_UNROLL)]
            for u in range(COMBINE_UNROLL):
                for k in range(TOP_K):
                    copy(rows[u][k], sl, k, r0 + u).start()
            return 0

        lax.fori_loop(0, tm // COMBINE_UNROLL, body, 0)

    @pl.when(i == 0)
    def _():
        issue(pos0_ref, 0)

    @pl.when(i + 1 < pl.num_programs(0))
    def _():
        issue(posn_ref, 1 - slot)

    pltpu.make_async_copy(ybuf.at[slot], ybuf.at[slot], sems.at[slot]).wait()
    gates = gate_ref[...]
    ff = jnp.zeros(h_ref.shape, F32)
    for k in range(TOP_K):
        y = jnp.concatenate([ybuf[slot, k, s] for s in range(ns)], axis=1)
        ff = ff + gates[:, k:k + 1] * y
    o_ref[...] = _layer_norm(DN_ALPHA * h_ref[...] + ff, g_ref[...], b_ref[...])


def _combine_call(ys3, pos, gates, h, g, b, tm=256):
    T, D = h.shape
    nt = T // tm
    ns = D // LANE
    row = lambda w: pl.BlockSpec((tm, w), lambda i: (i, 0))
    pos = _tile_major(pos, tm)
    smem = lambda f: pl.BlockSpec((1, 1, TOP_K * tm), lambda i: (f(i), 0, 0), memory_space=pltpu.SMEM)
    return pl.pallas_call(
        _combine_kernel,
        grid=(nt,),
        in_specs=[smem(lambda i: jnp.minimum(i + 1, nt - 1)), smem(lambda i: 0), row(LANE), row(D),
                  pl.BlockSpec(memory_space=pl.ANY), _full((1, D)), _full((1, D))],
        out_specs=row(D),
        out_shape=jax.ShapeDtypeStruct((T, D), F32),
        scratch_shapes=[pltpu.VMEM((2, TOP_K, ns, tm, LANE), F32), pltpu.SemaphoreType.DMA((2,))],
        compiler_params=_cparams(1),
        name="combine",
    )(pos, pos, gates, h, ys3, g.reshape(1, D), b.reshape(1, D))


def _rope_tables(S):
    pos = np.arange(S)
    row, col = pos // GRID_W, pos % GRID_W

    def tab(p, dim):
        inv = ROPE_THETA ** (-np.arange(0, dim, 2, dtype=np.float64) / dim)
        ang = p[:, None].astype(np.float64) * inv[None, :]
        return np.cos(ang), np.sin(ang)

    def pair(dim):
        cr, sr = tab(row, dim)
        cc, sc = tab(col, dim)
        return (np.concatenate([cr, cr, cc, cc], axis=1), np.concatenate([-sr, sr, -sc, sc], axis=1))

    cg, sg = pair(GQA_HEAD_DIM // 2)
    cg, sg = np.tile(cg, (1, 2)), np.tile(sg, (1, 2))
    cm32, sm32 = pair(MLA_ROPE // 2)
    cm = np.ones((S, LANE))
    sm = np.zeros((S, LANE))
    cm[:, MLA_NOPE:MLA_NOPE + MLA_ROPE] = cm32
    sm[:, MLA_NOPE:MLA_NOPE + MLA_ROPE] = sm32
    return tuple(jnp.asarray(a, F32) for a in (cg, sg, cm, sm))


def _pad_heads(w, n_heads, width, lo, hi):
    r = w.shape[0]
    w3 = w.reshape(r, n_heads, width)[:, :, lo:hi]
    w3 = jnp.pad(w3, ((0, 0), (0, 0), (0, HEAD_PAD - (hi - lo))))
    return w3.reshape(r, n_heads * HEAD_PAD).astype(BF16)


def kernel(x, mem, ln_in_g, ln_in_b, w_in, w_f, b_f, dw_w, dw_b, conv_ln_g, conv_ln_b, w_pw, b_pw, q_norm_g, k_norm_g, mla_q_norm_g, w_uq, mla_kv_norm_g, w_ukv, grp_norm_g, w_o, ln1_g, ln1_b, w_xq, w_xk, w_xv, w_xo, ln2_g, ln2_b, w_router, b_router, w_gu, b_gu, w_down, b_down, ln3_g, ln3_b):
    B, S, D = x.shape
    T = B * S
    depth = w_in.shape[0]
    tabs = _rope_tables(S)
    fconsts = _fnet_consts(S)
    lane = np.arange(LANE)
    gsum = jnp.asarray((lane[:, None] // GQA_HEAD_DIM) == (lane[None, :] // GQA_HEAD_DIM), BF16)
    mem2 = mem.reshape(B * mem.shape[1], D)
    kr_lo = w_in.shape[2] - MLA_ROPE

    h = _ln_call(x.reshape(T, D), ln_in_g, ln_in_b)
    for l in range(depth):
        zeros = lambda n: jnp.zeros((D, n), F32)
        w_in_p = jnp.concatenate([w_in[l][:, :kr_lo], zeros(MLA_NOPE), w_in[l][:, kr_lo:],
                                  zeros(LANE - MLA_NOPE - MLA_ROPE)], axis=1).astype(BF16)
        wuq = _pad_heads(w_uq[l], MLA_HEADS, MLA_NOPE + MLA_ROPE, 0, MLA_NOPE + MLA_ROPE)
        wukn = _pad_heads(w_ukv[l], MLA_HEADS, MLA_NOPE + MLA_V, 0, MLA_NOPE)
        wuv = _pad_heads(w_ukv[l], MLA_HEADS, MLA_NOPE + MLA_V, MLA_NOPE, MLA_NOPE + MLA_V)
        qg = jnp.tile(q_norm_g[l], 2).reshape(1, LANE)
        kg = jnp.tile(k_norm_g[l], 2).reshape(1, LANE)
        zf, u, qgq, kgk, vgv, qm, km, vm = _inproj_call(
            h, S, w_in_p, tabs, gsum, qg, kg, mla_q_norm_g[l].reshape(1, -1), mla_kv_norm_g[l].reshape(1, -1),
            wuq, wukn, wuv)

        yf = _fnet_call(zf, B, S, fconsts, w_f[l].astype(BF16), b_f[l])
        yc = _conv_call(u, B, S, dw_w[l], dw_b[l], conv_ln_g[l], conv_ln_b[l], w_pw[l].astype(BF16), b_pw[l])
        yg = _attn_call(qgq, kgk, vgv, B, S, (0, 0, 0, 0), (0, 0, 1, 1), "gqa_attn")
        ym = _attn_call(qm, km, vm, B, S, (0, 1, 2, 3), (0, 1, 2, 3), "mla_attn")
        h = _mixout_call((yf, yc, yg, ym), h, grp_norm_g[l], w_o[l].astype(BF16), ln1_g[l], ln1_b[l])

        kmem, vmem = _memkv_call(mem2, w_xk[l].astype(BF16), w_xv[l].astype(BF16))
        h, h3 = _xattn_call(h, B, S, w_xq[l].astype(BF16), kmem, vmem, w_xo[l].astype(BF16), ln2_g[l], ln2_b[l])

        w_r = jnp.pad(w_router[l], ((0, 0), (0, LANE - N_EXPERTS)))
        w_r_hi = w_r.astype(BF16)
        w_r_lo = (w_r - w_r_hi.astype(F32)).astype(BF16)
        b_r = jnp.concatenate([b_router[l], jnp.full((LANE - N_EXPERTS,), NEG_BIG, F32)]).reshape(1, LANE)
        idx_t, rank_t, gates, counts = _router_call(h, w_r_hi, w_r_lo, b_r)
        block_e, pos, meta = _moe_layout(counts, idx_t, rank_t, T)
        xs3 = _dispatch_call(h3, pos, meta, block_e.shape[0])
        ys3 = _moe_call(xs3, block_e, meta, l, w_gu, b_gu, w_down, b_down)
        h = _combine_call(ys3, pos, gates, h, ln3_g[l], ln3_b[l])
    return h.reshape(B, S, D)
```

```python
import functools
import math

import numpy as np
import jax
import jax.numpy as jnp
from jax import lax
from jax.experimental import pallas as pl
from jax.experimental.pallas import tpu as pltpu

F32 = jnp.float32
BF16 = jnp.bfloat16

D_MODEL = 1024
DEPTH = 2
GRID_W = 64
N_GROUPS = 4
GROUP_W = D_MODEL // N_GROUPS
FNET_CH = 64
CONV_K = 31
CONV_HALO = 16
GQA_HEAD_DIM = 64
GQA_Q_HEADS = 4
GQA_KV_HEADS = 2
MLA_HEADS = 4
MLA_NOPE = 64
MLA_ROPE = 32
MLA_V = 64
MLA_Q_RANK = 256
MLA_KV_RANK = 128
X_HEADS = 4
X_HEAD_DIM = D_MODEL // X_HEADS
N_EXPERTS = 32
TOP_K = 4
D_FF = D_MODEL
SWIGLU_LIMIT = 7.0
SWIGLU_ALPHA = 1.702
ROPE_THETA = 10000.0
LN_EPS = 1e-5
RMS_EPS = 1e-6
DN_ALPHA = (2 * DEPTH) ** 0.25

LANE = 128
HEAD_PAD = 128
IN_COLS_PAD = 1792
DFT_N1 = 64
MOE_BLOCK = 256
NEG_BIG = -1e30
LOG2_E = math.log2(math.e)
VMEM_LIMIT = 56 * 1024 * 1024


def _cparams(n_axes, vmem=VMEM_LIMIT):
    return pltpu.CompilerParams(dimension_semantics=("arbitrary",) * n_axes, vmem_limit_bytes=vmem)


def _full(shape):
    nd = len(shape)
    return pl.BlockSpec(shape, lambda *a: (0,) * nd)


def _dot(a, b):
    return jnp.dot(a, b, preferred_element_type=F32)


def _split(x):
    hi = x.astype(BF16)
    lo = (x - hi.astype(F32)).astype(BF16)
    return hi, lo


def _dot3(a_hi, a_lo, b_hi, b_lo):
    return _dot(a_hi, b_hi) + (_dot(a_hi, b_lo) + _dot(a_lo, b_hi))


def _layer_norm(x, g, b):
    mu = jnp.mean(x, axis=-1, keepdims=True)
    xc = x - mu
    var = jnp.mean(xc * xc, axis=-1, keepdims=True)
    return xc * lax.rsqrt(var + LN_EPS) * g + b


def _rms(x, g):
    return x * lax.rsqrt(jnp.mean(x * x, axis=-1, keepdims=True) + RMS_EPS) * g


def _rope(x, cos, sin_signed, half):
    lane = lax.broadcasted_iota(jnp.int32, x.shape, 1)
    first = (lane % (2 * half)) < half
    partner = jnp.where(first, pltpu.roll(x, LANE - half, 1), pltpu.roll(x, half, 1))
    return x * cos + partner * sin_signed


def _ln_kernel(x_ref, g_ref, b_ref, o_ref):
    o_ref[...] = _layer_norm(x_ref[...], g_ref[...], b_ref[...])


def _ln_call(x, g, b, tm=1024):
    T, D = x.shape
    return pl.pallas_call(
        _ln_kernel,
        grid=(T // tm,),
        in_specs=[pl.BlockSpec((tm, D), lambda i: (i, 0)), _full((1, D)), _full((1, D))],
        out_specs=pl.BlockSpec((tm, D), lambda i: (i, 0)),
        out_shape=jax.ShapeDtypeStruct((T, D), F32),
        compiler_params=_cparams(1),
        name="ln_in",
    )(x, g.reshape(1, D), b.reshape(1, D))


def _inproj_kernel(h_ref, w_ref, cg_ref, sg_ref, cm_ref, sm_ref, gsum_ref, qg_ref, kg_ref,
                   mqg_ref, mkg_ref, wuq_ref, wukn_ref, wuv_ref,
                   zf_ref, u_ref, qgo_ref, kgo_ref, vgo_ref, qmo_ref, kmo_ref, vmo_ref):
    z = _dot(h_ref[...].astype(BF16), w_ref[...])
    zf_ref[...] = z[:, 0:256]
    a = z[:, 256:512]
    g = z[:, 512:768]
    u_ref[...] = a * (1.0 / (1.0 + jnp.exp(-g)))

    cg = cg_ref[...]
    sg = sg_ref[...]
    cm = cm_ref[...]
    sm = sm_ref[...]
    lane = lax.broadcasted_iota(jnp.int32, cg.shape, 1)
    low = lane < GQA_HEAD_DIM
    gsum = gsum_ref[...]

    def head_norm(x, gain):
        hi, lo = _split(x * x)
        ssq = _dot(hi, gsum) + _dot(lo, gsum)
        return x * lax.rsqrt(ssq * (1.0 / GQA_HEAD_DIM) + RMS_EPS) * gain

    scale_g = GQA_HEAD_DIM ** -0.5 * LOG2_E
    q_tiles = []
    for t in range(2):
        x = z[:, 768 + t * LANE: 768 + (t + 1) * LANE]
        x = _rope(head_norm(x, qg_ref[...]), cg, sg, GQA_HEAD_DIM // 4) * scale_g
        xr = pltpu.roll(x, GQA_HEAD_DIM, 1)
        if t == 0:
            q_tiles += [jnp.where(low, x, 0.0), jnp.where(low, xr, 0.0)]
        else:
            q_tiles += [jnp.where(low, 0.0, xr), jnp.where(low, 0.0, x)]
    qgo_ref[...] = jnp.concatenate(q_tiles, axis=1).astype(BF16)

    k = z[:, 1024:1152]
    kgo_ref[...] = _rope(head_norm(k, kg_ref[...]), cg, sg, GQA_HEAD_DIM // 4).astype(BF16)

    v = z[:, 1152:1280]
    ones_col = jnp.where(lane == GQA_HEAD_DIM, 1.0, 0.0)
    v0 = jnp.where(low, v, ones_col)
    v1 = jnp.where(low, pltpu.roll(v, GQA_HEAD_DIM, 1), ones_col)
    vgo_ref[...] = jnp.concatenate([v0, v1], axis=1).astype(BF16)

    scale_m = (MLA_NOPE + MLA_ROPE) ** -0.5 * LOG2_E
    cq = _rms(z[:, 1280:1536], mqg_ref[...]).astype(BF16)
    qm = _dot(cq, wuq_ref[...])
    qm_tiles = [_rope(qm[:, t * LANE:(t + 1) * LANE], cm, sm, MLA_ROPE // 4) * scale_m
                for t in range(MLA_HEADS)]
    qmo_ref[...] = jnp.concatenate(qm_tiles, axis=1).astype(BF16)

    ckv = _rms(z[:, 1536:1664], mkg_ref[...]).astype(BF16)
    kr = _rope(z[:, 1664:1792], cm, sm, MLA_ROPE // 4)
    kn = _dot(ckv, wukn_ref[...])
    kmo_ref[...] = (kn + jnp.concatenate([kr] * MLA_HEADS, axis=1)).astype(BF16)
    vm = _dot(ckv, wuv_ref[...])
    vmo_ref[...] = (vm + jnp.concatenate([ones_col] * MLA_HEADS, axis=1)).astype(BF16)


def _inproj_call(h, S, w_in_p, tabs, gsum, qg, kg, mqg, mkg, wuq, wukn, wuv, tm=512):
    T, D = h.shape
    nsb = S // tm
    cg, sg, cm, sm = tabs
    row = lambda w: pl.BlockSpec((tm, w), lambda i: (i, 0))
    tab = pl.BlockSpec((tm, LANE), lambda i: (i % nsb, 0))
    outs = [(256, F32), (256, F32), (512, BF16), (128, BF16), (256, BF16), (512, BF16), (512, BF16), (512, BF16)]
    return pl.pallas_call(
        _inproj_kernel,
        grid=(T // tm,),
        in_specs=[row(D), _full(w_in_p.shape), tab, tab, tab, tab, _full(gsum.shape), _full(qg.shape),
                  _full(kg.shape), _full(mqg.shape), _full(mkg.shape), _full(wuq.shape), _full(wukn.shape),
                  _full(wuv.shape)],
        out_specs=[row(w) for w, _ in outs],
        out_shape=[jax.ShapeDtypeStruct((T, w), dt) for w, dt in outs],
        compiler_params=_cparams(1),
        name="in_proj",
    )(h, w_in_p, cg, sg, cm, sm, gsum, qg, kg, mqg, mkg, wuq, wukn, wuv)


def _fft1_kernel(x_ref, chi_ref, clo_ref, o_ref):
    chi = chi_ref[...]
    clo = clo_ref[...]
    for j in range(8):
        xh, xl = _split(x_ref[0, :, j, :])
        o_ref[0, :, j, :] = _dot3(chi, clo, xh, xl)


def _fft2_kernel(re_ref, im_ref, mhi_ref, mlo_ref, chh_ref, chl_ref, wf_ref, bf_ref, o_ref):
    n2 = re_ref.shape[2]
    for j in range(8):
        ab = jnp.concatenate([re_ref[0, j], im_ref[0, j]], axis=0)
        abh, abl = _split(ab)
        ri = _dot3(mhi_ref[j], mlo_ref[j], abh, abl)
        rc = jnp.concatenate([ri[:n2], ri[n2:]], axis=1)
        rch, rcl = _split(rc)
        y = _dot3(rch, rcl, chh_ref[...], chl_ref[...])
        o_ref[0, :, j, :] = _dot(y.astype(BF16), wf_ref[...]) + bf_ref[...]


def _fnet_call(zf, B, S, consts, w_f, b_f):
    C = GROUP_W
    n1, n2 = DFT_N1, S // DFT_N1
    c1h, c1l, mh, ml, chh, chl = consts
    x4 = zf.reshape(B, n1, n2, C)
    ab = pl.pallas_call(
        _fft1_kernel,
        grid=(B, n2 // 8),
        in_specs=[pl.BlockSpec((1, n1, 8, C), lambda b, i: (b, 0, i, 0)), _full(c1h.shape), _full(c1l.shape)],
        out_specs=pl.BlockSpec((1, 2 * n1, 8, C), lambda b, i: (b, 0, i, 0)),
        out_shape=jax.ShapeDtypeStruct((B, 2 * n1, n2, C), F32),
        compiler_params=_cparams(2),
        name="fnet_stage1",
    )(x4, c1h, c1l)
    g = n1 // 8
    y = pl.pallas_call(
        _fft2_kernel,
        grid=(B, g),
        in_specs=[pl.BlockSpec((1, 8, n2, C), lambda b, i: (b, i, 0, 0)),
                  pl.BlockSpec((1, 8, n2, C), lambda b, i: (b, i + g, 0, 0)),
                  pl.BlockSpec((8, 2 * n2, 2 * n2), lambda b, i: (i, 0, 0)),
                  pl.BlockSpec((8, 2 * n2, 2 * n2), lambda b, i: (i, 0, 0)),
                  _full(chh.shape), _full(chl.shape), _full(w_f.shape), _full((1, C))],
        out_specs=pl.BlockSpec((1, n2, 8, C), lambda b, i: (b, 0, i, 0)),
        out_shape=jax.ShapeDtypeStruct((B, n2, n1, C), F32),
        compiler_params=_cparams(2),
        name="fnet_stage2",
    )(ab, ab, mh, ml, chh, chl, w_f, b_f.reshape(1, C))
    return y.reshape(B * S, C)


def _fnet_consts(S):
    n1, n2 = DFT_N1, S // DFT_N1
    k1 = np.arange(n1)
    ang1 = 2.0 * np.pi * np.outer(k1, k1) / n1
    c1 = np.concatenate([np.cos(ang1), np.sin(ang1)], axis=0)
    kk = k1[:, None, None] + n1 * np.arange(n2)[None, :, None]
    ang = 2.0 * np.pi * ((kk * np.arange(n2)[None, None, :]) % S) / S
    ct, st = np.cos(ang), np.sin(ang)
    m = np.concatenate([np.concatenate([ct, -st], axis=2),
                        np.concatenate([st, ct], axis=2)], axis=1)
    kc = np.arange(FNET_CH)
    angc = 2.0 * np.pi * np.outer(kc, kc) / FNET_CH
    eye = np.eye(GROUP_W // FNET_CH)
    ch = np.concatenate([np.kron(eye, np.cos(angc)), -np.kron(eye, np.sin(angc))], axis=0)

    def split(a):
        a32 = jnp.asarray(a, F32)
        hi = a32.astype(BF16)
        return hi, (a32 - hi.astype(F32)).astype(BF16)

    return split(c1) + split(m) + split(ch)


def _conv_kernel(prev_ref, cur_ref, next_ref, w_ref, b_ref, g_ref, beta_ref, wpw_ref, bpw_ref, o_ref, ext_ref):
    i = pl.program_id(1)
    ts = cur_ref.shape[1]
    H = CONV_HALO
    ext_ref[0:H, :] = jnp.where(i > 0, prev_ref[0], 0.0)
    ext_ref[H:H + ts, :] = cur_ref[0]
    ext_ref[H + ts:H + ts + H, :] = jnp.where(i < pl.num_programs(1) - 1, next_ref[0], 0.0)
    acc = jnp.zeros((ts, cur_ref.shape[2]), F32)
    for k in range(CONV_K):
        off = H - CONV_K // 2 + k
        acc = acc + ext_ref[off:off + ts, :] * w_ref[k:k + 1, :]
    y = _layer_norm(acc + b_ref[...], g_ref[...], beta_ref[...])
    y = y * (1.0 / (1.0 + jnp.exp(-y)))
    o_ref[...] = _dot(y.astype(BF16), wpw_ref[...]) + bpw_ref[...]


def _conv_call(u, B, S, dw_w, dw_b, cg, cb, w_pw, b_pw, ts=512):
    C = GROUP_W
    H = CONV_HALO
    nb = S // ts
    r = ts // H
    u3 = u.reshape(B, S, C)
    vec = lambda a: a.reshape(1, C)
    return pl.pallas_call(
        _conv_kernel,
        grid=(B, nb),
        in_specs=[pl.BlockSpec((1, H, C), lambda b, i: (b, jnp.maximum(i * r - 1, 0), 0)),
                  pl.BlockSpec((1, ts, C), lambda b, i: (b, i, 0)),
                  pl.BlockSpec((1, H, C), lambda b, i: (b, jnp.minimum((i + 1) * r, S // H - 1), 0)),
                  _full((CONV_K, C)), _full((1, C)), _full((1, C)), _full((1, C)), _full((C, C)), _full((1, C))],
        out_specs=pl.BlockSpec((ts, C), lambda b, i: (b * nb + i, 0)),
        out_shape=jax.ShapeDtypeStruct((B * S, C), F32),
        scratch_shapes=[pltpu.VMEM((ts + 2 * H, C), F32)],
        compiler_params=_cparams(2),
        name="conv_module",
    )(u3, u3, u3, dw_w, vec(dw_b), vec(cg), vec(cb), w_pw, vec(b_pw))


def _attn_kernel(q_ref, k_ref, v_ref, o_ref, *, k_tile, v_tile, tk, n_sub):
    tq = q_ref.shape[1]
    S = k_ref.shape[1]
    dv = GQA_HEAD_DIM
    sub = tq // n_sub
    outs = []
    for h in range(len(k_tile)):
        qs = [q_ref[0, i * sub:(i + 1) * sub, h * LANE:(h + 1) * LANE] for i in range(n_sub)]
        kb, vb = k_tile[h], v_tile[h]

        def body(c, carry):
            off = pl.multiple_of(c * tk, tk)
            k = k_ref[0, pl.ds(off, tk), kb * LANE:(kb + 1) * LANE]
            v = v_ref[0, pl.ds(off, tk), vb * LANE:(vb + 1) * LANE]
            new = []
            for q, (m, acc) in zip(qs, carry):
                s = lax.dot_general(q, k, (((1,), (1,)), ((), ())), preferred_element_type=F32)
                m_new = jnp.maximum(m, jnp.max(s, axis=-1, keepdims=True))
                p = jnp.exp2(s - m_new)
                acc = jnp.exp2(m - m_new) * acc + _dot(p.astype(BF16), v)
                new.append((m_new, acc))
            return tuple(new)

        init = tuple((jnp.full((sub, 1), NEG_BIG, F32), jnp.zeros((sub, LANE), F32)) for _ in range(n_sub))
        fin = lax.fori_loop(0, S // tk, body, init)
        acc = jnp.concatenate([a for _, a in fin], axis=0)
        outs.append(acc[:, :dv] / acc[:, dv:dv + 1])
    o_ref[0] = jnp.concatenate(outs, axis=1)


def _attn_call(q, k, v, B, S, k_tile, v_tile, name, tq=1024, tk=2048, n_sub=1):
    nh = len(k_tile)
    q3 = q.reshape(B, S, q.shape[-1])
    k3 = k.reshape(B, S, k.shape[-1])
    v3 = v.reshape(B, S, v.shape[-1])
    kern = functools.partial(_attn_kernel, k_tile=k_tile, v_tile=v_tile, tk=tk, n_sub=n_sub)
    out = pl.pallas_call(
        kern,
        grid=(B, S // tq),
        in_specs=[pl.BlockSpec((1, tq, q3.shape[-1]), lambda b, i: (b, i, 0)),
                  pl.BlockSpec((1, S, k3.shape[-1]), lambda b, i: (b, 0, 0), pipeline_mode=pl.Buffered(1)),
                  pl.BlockSpec((1, S, v3.shape[-1]), lambda b, i: (b, 0, 0), pipeline_mode=pl.Buffered(1))],
        out_specs=pl.BlockSpec((1, tq, nh * GQA_HEAD_DIM), lambda b, i: (b, i, 0)),
        out_shape=jax.ShapeDtypeStruct((B, S, nh * GQA_HEAD_DIM), F32),
        compiler_params=_cparams(2),
        name=name,
    )(q3, k3, v3)
    return out.reshape(B * S, nh * GQA_HEAD_DIM)


def _mixout_kernel(yf_ref, yc_ref, yg_ref, ym_ref, h_ref, gg_ref, wo_ref, g_ref, b_ref, o_ref):
    parts = [_rms(r[...], gg_ref[i:i + 1, :]).astype(BF16)
             for i, r in enumerate((yf_ref, yc_ref, yg_ref, ym_ref))]
    mix = _dot(jnp.concatenate(parts, axis=1), wo_ref[...])
    o_ref[...] = _layer_norm(DN_ALPHA * h_ref[...] + mix, g_ref[...], b_ref[...])


def _mixout_call(ys, h, grp_g, w_o, g, b, tm=512):
    T, D = h.shape
    C = GROUP_W
    row = lambda w: pl.BlockSpec((tm, w), lambda i: (i, 0))
    return pl.pallas_call(
        _mixout_kernel,
        grid=(T // tm,),
        in_specs=[row(C)] * 4 + [row(D), _full((N_GROUPS, C)), _full((D, D)), _full((1, D)), _full((1, D))],
        out_specs=row(D),
        out_shape=jax.ShapeDtypeStruct((T, D), F32),
        compiler_params=_cparams(1),
        name="mix_out",
    )(*ys, h, grp_g, w_o, g.reshape(1, D), b.reshape(1, D))


def _memkv_kernel(m_ref, wk_ref, wv_ref, k_ref, v_ref):
    m = m_ref[...].astype(BF16)
    k_ref[...] = _dot(m, wk_ref[...]).astype(BF16)
    v_ref[...] = _dot(m, wv_ref[...]).astype(BF16)


def _memkv_call(mem2, w_xk, w_xv, tm=256):
    R, D = mem2.shape
    row = pl.BlockSpec((tm, D), lambda i: (i, 0))
    return pl.pallas_call(
        _memkv_kernel,
        grid=(R // tm,),
        in_specs=[row, _full((D, D)), _full((D, D))],
        out_specs=[row, row],
        out_shape=[jax.ShapeDtypeStruct((R, D), BF16)] * 2,
        compiler_params=_cparams(1),
        name="mem_kv",
    )(mem2, w_xk, w_xv)


def _xattn_kernel(h_ref, wq_ref, k_ref, v_ref, wo_ref, g_ref, b_ref, o_ref, o3_ref):
    h = h_ref[0]
    q = (_dot(h.astype(BF16), wq_ref[...]) * (X_HEAD_DIM ** -0.5)).astype(BF16)
    outs = []
    for hd in range(X_HEADS):
        sl = slice(hd * X_HEAD_DIM, (hd + 1) * X_HEAD_DIM)
        s = lax.dot_general(q[:, sl], k_ref[0, :, sl], (((1,), (1,)), ((), ())), preferred_element_type=F32)
        e = jnp.exp(s - jnp.max(s, axis=-1, keepdims=True))
        o = _dot(e.astype(BF16), v_ref[0, :, sl]) / jnp.sum(e, axis=-1, keepdims=True)
        outs.append(o.astype(BF16))
    xa = _dot(jnp.concatenate(outs, axis=1), wo_ref[...])
    hn = _layer_norm(DN_ALPHA * h + xa, g_ref[...], b_ref[...])
    o_ref[0] = hn
    tm = hn.shape[0]
    ns = D_MODEL // LANE
    for s in range(ns):
        o3_ref[pl.ds(s, tm, stride=ns), :] = hn[:, s * LANE:(s + 1) * LANE]


def _xattn_call(h, B, S, w_xq, kmem, vmem, w_xo, g, b, tm=256):
    D = D_MODEL
    M = kmem.shape[0] // B
    ns = D // LANE
    nsb = S // tm
    row = pl.BlockSpec((1, tm, D), lambda bb, i: (bb, i, 0))
    kv = pl.BlockSpec((1, M, D), lambda bb, i: (bb, 0, 0))
    out, out3 = pl.pallas_call(
        _xattn_kernel,
        grid=(B, nsb),
        in_specs=[row, _full((D, D)), kv, kv, _full((D, D)), _full((1, D)), _full((1, D))],
        out_specs=[row, pl.BlockSpec((tm * ns, LANE), lambda bb, i: (bb * nsb + i, 0))],
        out_shape=[jax.ShapeDtypeStruct((B, S, D), F32), jax.ShapeDtypeStruct((B * S * ns, LANE), F32)],
        compiler_params=_cparams(2),
        name="cross_attn",
    )(h.reshape(B, S, D), w_xq, kmem.reshape(B, M, D), vmem.reshape(B, M, D), w_xo, g.reshape(1, D), b.reshape(1, D))
    return out.reshape(B * S, D), out3.reshape(B * S, ns, LANE)


def _router_kernel(h_ref, whi_ref, wlo_ref, b_ref, tri_ref, idx_ref, rank_ref, gate_ref, cnt_ref, carry_ref):
    @pl.when(pl.program_id(0) == 0)
    def _():
        carry_ref[...] = jnp.zeros(carry_ref.shape, F32)

    hh, hl = _split(h_ref[...])
    logits = _dot3(hh, hl, whi_ref[...], wlo_ref[...]) + b_ref[...]
    lane = lax.broadcasted_iota(jnp.int32, logits.shape, 1)
    idx_out = jnp.zeros(logits.shape, F32)
    val_out = jnp.zeros(logits.shape, F32)
    onehot = jnp.zeros(logits.shape, F32)
    picks = []
    top = None
    denom = jnp.zeros((logits.shape[0], 1), F32)
    for k in range(TOP_K):
        mx = jnp.max(logits, axis=-1, keepdims=True)
        pick = jnp.min(jnp.where(logits == mx, lane, LANE), axis=-1, keepdims=True)
        if k == 0:
            top = mx
        e = jnp.exp(mx - top)
        denom = denom + e
        hit = lane == pick
        idx_out = jnp.where(lane == k, pick.astype(F32), idx_out)
        val_out = jnp.where(lane == k, e, val_out)
        onehot = jnp.where(hit, 1.0, onehot)
        logits = jnp.where(hit, NEG_BIG * 2.0, logits)
        picks.append(pick)
    gate_ref[...] = val_out / denom

    before = _dot(tri_ref[...], onehot.astype(BF16)) + carry_ref[0:1, :]
    rank_out = jnp.zeros(logits.shape, F32)
    for k, pick in enumerate(picks):
        r = jnp.sum(jnp.where(lane == pick, before, 0.0), axis=-1, keepdims=True)
        rank_out = jnp.where(lane == k, r, rank_out)
    total = carry_ref[0:1, :] + jnp.sum(onehot, axis=0, keepdims=True)
    carry_ref[...] = jnp.broadcast_to(total, carry_ref.shape)
    cnt_ref[...] = jnp.broadcast_to(total, cnt_ref.shape)
    idx_ref[...] = jnp.transpose(idx_out)[:8, :].astype(jnp.int32)
    rank_ref[...] = jnp.transpose(rank_out)[:8, :].astype(jnp.int32)


def _router_call(h, w_hi, w_lo, b_pad, tm=1024):
    T, D = h.shape
    tri = jnp.asarray(np.tril(np.ones((tm, tm), np.float32), -1), BF16)
    row = lambda w: pl.BlockSpec((tm, w), lambda i: (i, 0))
    col = pl.BlockSpec((8, tm), lambda i: (0, i))
    return pl.pallas_call(
        _router_kernel,
        grid=(T // tm,),
        in_specs=[row(D), _full((D, LANE)), _full((D, LANE)), _full((1, LANE)), _full((tm, tm))],
        out_specs=[col, col, row(LANE), _full((8, LANE))],
        out_shape=[jax.ShapeDtypeStruct((8, T), jnp.int32), jax.ShapeDtypeStruct((8, T), jnp.int32),
                   jax.ShapeDtypeStruct((T, LANE), F32), jax.ShapeDtypeStruct((8, LANE), F32)],
        scratch_shapes=[pltpu.VMEM((8, LANE), F32)],
        compiler_params=_cparams(1),
        name="router",
    )(h, w_hi, w_lo, b_pad, tri)


def _moe_layout(counts, idx_t, rank_t, T):
    nb = T * TOP_K // MOE_BLOCK + N_EXPERTS
    cnt = counts[0, :N_EXPERTS].astype(jnp.int32)
    nblk = (cnt + MOE_BLOCK - 1) // MOE_BLOCK
    bend = jnp.cumsum(nblk)
    bstart = bend - nblk
    blk = jnp.arange(nb, dtype=jnp.int32)
    block_e = jnp.minimum(jnp.sum((blk[:, None] >= bend[None, :]).astype(jnp.int32), axis=1), N_EXPERTS - 1)
    first_row = bstart * MOE_BLOCK
    e4 = idx_t[:TOP_K]
    base = jnp.zeros_like(e4)
    for e in range(N_EXPERTS):
        base = jnp.where(e4 == e, first_row[e], base)
    pos = (base + rank_t[:TOP_K]).astype(jnp.int32)
    meta = jnp.concatenate([jnp.where(nblk > 0, bend - 1, -1), bend[-1:]]).astype(jnp.int32)
    return block_e.astype(jnp.int32), pos, meta


DISPATCH_TOKENS = 1024
DISPATCH_UNROLL = 8


def _dispatch_kernel(meta_ref, pos_ref, h_ref, x_hbm, zbuf, sem):
    i = pl.program_id(0)
    td = h_ref.shape[0]
    nb = x_hbm.shape[0] // MOE_BLOCK

    def zero_block(b):
        return pltpu.make_async_copy(zbuf, x_hbm.at[pl.ds(b * MOE_BLOCK, MOE_BLOCK)], sem)

    @pl.when(i == 0)
    def _():
        zbuf[...] = jnp.zeros(zbuf.shape, F32)
        used = meta_ref[N_EXPERTS]
        for phase in ("start", "wait"):
            for e in range(N_EXPERTS):
                @pl.when(meta_ref[e] >= 0)
                def _():
                    getattr(zero_block(meta_ref[e]), phase)()

            def tail(b, _):
                getattr(zero_block(b), phase)()
                return 0

            lax.fori_loop(used, nb, tail, 0)

    def body(g, _):
        t0 = g * DISPATCH_UNROLL
        rows = [[pos_ref[0, 0, k * td + t0 + u] for k in range(TOP_K)] for u in range(DISPATCH_UNROLL)]
        for u in range(DISPATCH_UNROLL):
            for k in range(TOP_K):
                pltpu.make_async_copy(h_ref.at[t0 + u], x_hbm.at[rows[u][k]], sem).start(priority=k % 2)
        return 0

    lax.fori_loop(0, td // DISPATCH_UNROLL, body, 0)
    for _ in range(TOP_K):
        pltpu.make_async_copy(h_ref, x_hbm.at[pl.ds(0, td)], sem).wait()


def _dispatch_call(h3, pos, meta, nb):
    T = h3.shape[0]
    td = DISPATCH_TOKENS
    grid_spec = pltpu.PrefetchScalarGridSpec(
        num_scalar_prefetch=1,
        grid=(T // td,),
        in_specs=[pl.BlockSpec((1, 1, TOP_K * td), lambda i, m: (i, 0, 0), memory_space=pltpu.SMEM),
                  pl.BlockSpec((td,) + h3.shape[1:], lambda i, m: (i, 0, 0))],
        out_specs=pl.BlockSpec(memory_space=pl.ANY),
        scratch_shapes=[pltpu.VMEM((MOE_BLOCK,) + h3.shape[1:], F32), pltpu.SemaphoreType.DMA(())],
    )
    return pl.pallas_call(
        _dispatch_kernel,
        grid_spec=grid_spec,
        out_shape=jax.ShapeDtypeStruct((nb * MOE_BLOCK,) + h3.shape[1:], F32),
        compiler_params=_cparams(1),
        name="dispatch",
    )(meta, _tile_major(pos, td), h3)


MOE_CHUNKS = 4


def _moe_kernel(be_ref, meta_ref, x_ref, wgu_ref, bgu_ref, wdn_ref, bdn_ref, o_ref, wgu_bf, wdn_bf, act_ref):
    j = pl.program_id(0)
    ns = D_MODEL // LANE

    @pl.when((j == 0) | (be_ref[j] != be_ref[jnp.maximum(j - 1, 0)]))
    def _():
        wgu_bf[...] = wgu_ref[0, 0].astype(BF16)
        wdn_bf[...] = wdn_ref[0, 0].astype(BF16)

    @pl.when(j < meta_ref[N_EXPERTS])
    def _():
        x = jnp.concatenate([x_ref[pl.ds(s, MOE_BLOCK, stride=ns), :] for s in range(ns)], axis=1).astype(BF16)
        cw = D_FF // MOE_CHUNKS
        for c in range(MOE_CHUNKS):
            gate = _dot(x, wgu_bf[:, c * cw:(c + 1) * cw]) + bgu_ref[0, 0, :, c * cw:(c + 1) * cw]
            up = _dot(x, wgu_bf[:, D_FF + c * cw:D_FF + (c + 1) * cw]) + bgu_ref[0, 0, :, D_FF + c * cw:D_FF + (c + 1) * cw]
            gate = jnp.minimum(gate, SWIGLU_LIMIT)
            up = jnp.clip(up, -SWIGLU_LIMIT, SWIGLU_LIMIT)
            act = (up + 1.0) * (gate * (1.0 / (1.0 + jnp.exp(-SWIGLU_ALPHA * gate))))
            act_ref[:, c * cw:(c + 1) * cw] = act.astype(BF16)
        ow = D_MODEL // MOE_CHUNKS
        for n in range(MOE_CHUNKS):
            yn = _dot(act_ref[...], wdn_bf[:, n * ow:(n + 1) * ow]) + bdn_ref[0, 0, :, n * ow:(n + 1) * ow]
            for t in range(ow // LANE):
                s = n * (ow // LANE) + t
                o_ref[pl.ds(s, MOE_BLOCK, stride=ns), :] = yn[:, t * LANE:(t + 1) * LANE]

    @pl.when(j >= meta_ref[N_EXPERTS])
    def _():
        o_ref[...] = jnp.zeros(o_ref.shape, F32)


def _moe_call(xs3, block_e, meta, layer, w_gu, b_gu, w_down, b_down):
    D = D_MODEL
    ns = D // LANE
    nb = block_e.shape[0]
    L, E = w_gu.shape[:2]
    rows = MOE_BLOCK * ns
    by_e = lambda shp: pl.BlockSpec((1, 1) + shp, lambda j, be, m: (layer, be[j], 0, 0))
    grid_spec = pltpu.PrefetchScalarGridSpec(
        num_scalar_prefetch=2,
        grid=(nb,),
        in_specs=[pl.BlockSpec((rows, LANE), lambda j, be, m: (jnp.minimum(j, m[N_EXPERTS] - 1), 0)),
                  by_e((D, 2 * D_FF)), by_e((1, 2 * D_FF)), by_e((D_FF, D)), by_e((1, D))],
        out_specs=pl.BlockSpec((rows, LANE), lambda j, be, m: (j, 0)),
        scratch_shapes=[pltpu.VMEM((D, 2 * D_FF), BF16), pltpu.VMEM((D_FF, D), BF16),
                        pltpu.VMEM((MOE_BLOCK, D_FF), BF16)],
    )
    ys = pl.pallas_call(
        _moe_kernel,
        grid_spec=grid_spec,
        out_shape=jax.ShapeDtypeStruct((nb * rows, LANE), F32),
        compiler_params=_cparams(1),
        name="routed_experts",
    )(block_e, meta, xs3.reshape(nb * rows, LANE), w_gu, b_gu.reshape(L, E, 1, 2 * D_FF), w_down,
      b_down.reshape(L, E, 1, D))
    return ys.reshape(xs3.shape)


COMBINE_UNROLL = 8


def _tile_major(pos, tile):
    T = pos.shape[1]
    return pos.reshape(TOP_K, T // tile, tile).transpose(1, 0, 2).reshape(T // tile, 1, TOP_K * tile)


def _combine_kernel(posn_ref, pos0_ref, gate_ref, h_ref, y_hbm, g_ref, b_ref, o_ref, ybuf, sems):
    i = pl.program_id(0)
    tm = h_ref.shape[0]
    ns = D_MODEL // LANE
    slot = i % 2

    def copy(row, sl, k, r):
        return pltpu.make_async_copy(y_hbm.at[row], ybuf.at[sl, k, :, r], sems.at[sl])

    def issue(pos_ref, sl):
        def body(g, _):
            r0 = g * COMBINE_UNROLL
            rows = [[pos_ref[0, 0, k * tm + r0 + u] for k in range(TOP_K)] for u in range(COMBINE_UNROLL)]
            for u in range(COMBINE_UNROLL):
                for k in range(TOP_K):
                    copy(rows[u][k], sl, k, r0 + u).start(priority=k % 2)
            return 0

        lax.fori_loop(0, tm // COMBINE_UNROLL, body, 0)

    @pl.when(i == 0)
    def _():
        issue(pos0_ref, 0)

    @pl.when(i + 1 < pl.num_programs(0))
    def _():
        issue(posn_ref, 1 - slot)

    pltpu.make_async_copy(ybuf.at[slot], ybuf.at[slot], sems.at[slot]).wait()
    gates = gate_ref[...]
    ff = jnp.zeros(h_ref.shape, F32)
    for k in range(TOP_K):
        y = jnp.concatenate([ybuf[slot, k, s] for s in range(ns)], axis=1)
        ff = ff + gates[:, k:k + 1] * y
    o_ref[...] = _layer_norm(DN_ALPHA * h_ref[...] + ff, g_ref[...], b_ref[...])


def _combine_call(ys3, pos, gates, h, g, b, tm=256):
    T, D = h.shape
    nt = T // tm
    ns = D // LANE
    row = lambda w: pl.BlockSpec((tm, w), lambda i: (i, 0))
    pos = _tile_major(pos, tm)
    smem = lambda f: pl.BlockSpec((1, 1, TOP_K * tm), lambda i: (f(i), 0, 0), memory_space=pltpu.SMEM)
    return pl.pallas_call(
        _combine_kernel,
        grid=(nt,),
        in_specs=[smem(lambda i: jnp.minimum(i + 1, nt - 1)), smem(lambda i: 0), row(LANE), row(D),
                  pl.BlockSpec(memory_space=pl.ANY), _full((1, D)), _full((1, D))],
        out_specs=row(D),
        out_shape=jax.ShapeDtypeStruct((T, D), F32),
        scratch_shapes=[pltpu.VMEM((2, TOP_K, ns, tm, LANE), F32), pltpu.SemaphoreType.DMA((2,))],
        compiler_params=_cparams(1),
        name="combine",
    )(pos, pos, gates, h, ys3, g.reshape(1, D), b.reshape(1, D))


def _rope_tables(S):
    pos = np.arange(S)
    row, col = pos // GRID_W, pos % GRID_W

    def tab(p, dim):
        inv = ROPE_THETA ** (-np.arange(0, dim, 2, dtype=np.float64) / dim)
        ang = p[:, None].astype(np.float64) * inv[None, :]
        return np.cos(ang), np.sin(ang)

    def pair(dim):
        cr, sr = tab(row, dim)
        cc, sc = tab(col, dim)
        return (np.concatenate([cr, cr, cc, cc], axis=1), np.concatenate([-sr, sr, -sc, sc], axis=1))

    cg, sg = pair(GQA_HEAD_DIM // 2)
    cg, sg = np.tile(cg, (1, 2)), np.tile(sg, (1, 2))
    cm32, sm32 = pair(MLA_ROPE // 2)
    cm = np.ones((S, LANE))
    sm = np.zeros((S, LANE))
    cm[:, MLA_NOPE:MLA_NOPE + MLA_ROPE] = cm32
    sm[:, MLA_NOPE:MLA_NOPE + MLA_ROPE] = sm32
    return tuple(jnp.asarray(a, F32) for a in (cg, sg, cm, sm))


def _pad_heads(w, n_heads, width, lo, hi):
    r = w.shape[0]
    w3 = w.reshape(r, n_heads, width)[:, :, lo:hi]
    w3 = jnp.pad(w3, ((0, 0), (0, 0), (0, HEAD_PAD - (hi - lo))))
    return w3.reshape(r, n_heads * HEAD_PAD).astype(BF16)


def kernel(x, mem, ln_in_g, ln_in_b, w_in, w_f, b_f, dw_w, dw_b, conv_ln_g, conv_ln_b, w_pw, b_pw, q_norm_g, k_norm_g, mla_q_norm_g, w_uq, mla_kv_norm_g, w_ukv, grp_norm_g, w_o, ln1_g, ln1_b, w_xq, w_xk, w_xv, w_xo, ln2_g, ln2_b, w_router, b_router, w_gu, b_gu, w_down, b_down, ln3_g, ln3_b):
    B, S, D = x.shape
    T = B * S
    depth = w_in.shape[0]
    tabs = _rope_tables(S)
    fconsts = _fnet_consts(S)
    lane = np.arange(LANE)
    gsum = jnp.asarray((lane[:, None] // GQA_HEAD_DIM) == (lane[None, :] // GQA_HEAD_DIM), BF16)
    mem2 = mem.reshape(B * mem.shape[1], D)
    kr_lo = w_in.shape[2] - MLA_ROPE

    h = _ln_call(x.reshape(T, D), ln_in_g, ln_in_b)
    for l in range(depth):
        zeros = lambda n: jnp.zeros((D, n), F32)
        w_in_p = jnp.concatenate([w_in[l][:, :kr_lo], zeros(MLA_NOPE), w_in[l][:, kr_lo:],
                                  zeros(LANE - MLA_NOPE - MLA_ROPE)], axis=1).astype(BF16)
        wuq = _pad_heads(w_uq[l], MLA_HEADS, MLA_NOPE + MLA_ROPE, 0, MLA_NOPE + MLA_ROPE)
        wukn = _pad_heads(w_ukv[l], MLA_HEADS, MLA_NOPE + MLA_V, 0, MLA_NOPE)
        wuv = _pad_heads(w_ukv[l], MLA_HEADS, MLA_NOPE + MLA_V, MLA_NOPE, MLA_NOPE + MLA_V)
        qg = jnp.tile(q_norm_g[l], 2).reshape(1, LANE)
        kg = jnp.tile(k_norm_g[l], 2).reshape(1, LANE)
        zf, u, qgq, kgk, vgv, qm, km, vm = _inproj_call(
            h, S, w_in_p, tabs, gsum, qg, kg, mla_q_norm_g[l].reshape(1, -1), mla_kv_norm_g[l].reshape(1, -1),
            wuq, wukn, wuv)

        yf = _fnet_call(zf, B, S, fconsts, w_f[l].astype(BF16), b_f[l])
        yc = _conv_call(u, B, S, dw_w[l], dw_b[l], conv_ln_g[l], conv_ln_b[l], w_pw[l].astype(BF16), b_pw[l])
        yg = _attn_call(qgq, kgk, vgv, B, S, (0, 0, 0, 0), (0, 0, 1, 1), "gqa_attn")
        ym = _attn_call(qm, km, vm, B, S, (0, 1, 2, 3), (0, 1, 2, 3), "mla_attn")
        h = _mixout_call((yf, yc, yg, ym), h, grp_norm_g[l], w_o[l].astype(BF16), ln1_g[l], ln1_b[l])

        kmem, vmem = _memkv_call(mem2, w_xk[l].astype(BF16), w_xv[l].astype(BF16))
        h, h3 = _xattn_call(h, B, S, w_xq[l].astype(BF16), kmem, vmem, w_xo[l].astype(BF16), ln2_g[l], ln2_b[l])

        w_r = jnp.pad(w_router[l], ((0, 0), (0, LANE - N_EXPERTS)))
        w_r_hi = w_r.astype(BF16)
        w_r_lo = (w_r - w_r_hi.astype(F32)).astype(BF16)
        b_r = jnp.concatenate([b_router[l], jnp.full((LANE - N_EXPERTS,), NEG_BIG, F32)]).reshape(1, LANE)
        idx_t, rank_t, gates, counts = _router_call(h, w_r_hi, w_r_lo, b_r)
        block_e, pos, meta = _moe_layout(counts, idx_t, rank_t, T)
        xs3 = _dispatch_call(h3, pos, meta, block_e.shape[0])
        ys3 = _moe_call(xs3, block_e, meta, l, w_gu, b_gu, w_down, b_down)
        h = _combine_call(ys3, pos, gates, h, ln3_g[l], ln3_b[l])
    return h.reshape(B, S, D)
```

```python
import functools
import math

import numpy as np
import jax
import jax.numpy as jnp
from jax import lax
from jax.experimental import pallas as pl
from jax.experimental.pallas import tpu as pltpu

F32 = jnp.float32
BF16 = jnp.bfloat16

D_MODEL = 1024
DEPTH = 2
GRID_W = 64
N_GROUPS = 4
GROUP_W = D_MODEL // N_GROUPS
FNET_CH = 64
CONV_K = 31
CONV_HALO = 16
GQA_HEAD_DIM = 64
GQA_Q_HEADS = 4
GQA_KV_HEADS = 2
MLA_HEADS = 4
MLA_NOPE = 64
MLA_ROPE = 32
MLA_V = 64
MLA_Q_RANK = 256
MLA_KV_RANK = 128
X_HEADS = 4
X_HEAD_DIM = D_MODEL // X_HEADS
N_EXPERTS = 32
TOP_K = 4
D_FF = D_MODEL
SWIGLU_LIMIT = 7.0
SWIGLU_ALPHA = 1.702
ROPE_THETA = 10000.0
LN_EPS = 1e-5
RMS_EPS = 1e-6
DN_ALPHA = (2 * DEPTH) ** 0.25

LANE = 128
HEAD_PAD = 128
IN_COLS_PAD = 1792
DFT_N1 = 64
MOE_BLOCK = 256
NEG_BIG = -1e30
LOG2_E = math.log2(math.e)
VMEM_LIMIT = 56 * 1024 * 1024


def _cparams(n_axes, vmem=VMEM_LIMIT):
    return pltpu.CompilerParams(dimension_semantics=("arbitrary",) * n_axes, vmem_limit_bytes=vmem)


def _full(shape):
    nd = len(shape)
    return pl.BlockSpec(shape, lambda *a: (0,) * nd)


def _dot(a, b):
    return jnp.dot(a, b, preferred_element_type=F32)


def _split(x):
    hi = x.astype(BF16)
    lo = (x - hi.astype(F32)).astype(BF16)
    return hi, lo


def _dot3(a_hi, a_lo, b_hi, b_lo):
    return _dot(a_hi, b_hi) + (_dot(a_hi, b_lo) + _dot(a_lo, b_hi))


def _layer_norm(x, g, b):
    mu = jnp.mean(x, axis=-1, keepdims=True)
    xc = x - mu
    var = jnp.mean(xc * xc, axis=-1, keepdims=True)
    return xc * lax.rsqrt(var + LN_EPS) * g + b


def _rms(x, g):
    return x * lax.rsqrt(jnp.mean(x * x, axis=-1, keepdims=True) + RMS_EPS) * g


def _rope(x, cos, sin_signed, half):
    lane = lax.broadcasted_iota(jnp.int32, x.shape, 1)
    first = (lane % (2 * half)) < half
    partner = jnp.where(first, pltpu.roll(x, LANE - half, 1), pltpu.roll(x, half, 1))
    return x * cos + partner * sin_signed


def _ln_kernel(x_ref, g_ref, b_ref, o_ref):
    o_ref[...] = _layer_norm(x_ref[...], g_ref[...], b_ref[...])


def _ln_call(x, g, b, tm=1024):
    T, D = x.shape
    return pl.pallas_call(
        _ln_kernel,
        grid=(T // tm,),
        in_specs=[pl.BlockSpec((tm, D), lambda i: (i, 0)), _full((1, D)), _full((1, D))],
        out_specs=pl.BlockSpec((tm, D), lambda i: (i, 0)),
        out_shape=jax.ShapeDtypeStruct((T, D), F32),
        compiler_params=_cparams(1),
        name="ln_in",
    )(x, g.reshape(1, D), b.reshape(1, D))


def _inproj_kernel(h_ref, w_ref, cg_ref, sg_ref, cm_ref, sm_ref, gsum_ref, qg_ref, kg_ref,
                   mqg_ref, mkg_ref, wuq_ref, wukn_ref, wuv_ref,
                   zf_ref, u_ref, qgo_ref, kgo_ref, vgo_ref, qmo_ref, kmo_ref, vmo_ref):
    z = _dot(h_ref[...].astype(BF16), w_ref[...])
    zf_ref[...] = z[:, 0:256]
    a = z[:, 256:512]
    g = z[:, 512:768]
    u_ref[...] = a * (1.0 / (1.0 + jnp.exp(-g)))

    cg = cg_ref[...]
    sg = sg_ref[...]
    cm = cm_ref[...]
    sm = sm_ref[...]
    lane = lax.broadcasted_iota(jnp.int32, cg.shape, 1)
    low = lane < GQA_HEAD_DIM
    gsum = gsum_ref[...]

    def head_norm(x, gain):
        hi, lo = _split(x * x)
        ssq = _dot(hi, gsum) + _dot(lo, gsum)
        return x * lax.rsqrt(ssq * (1.0 / GQA_HEAD_DIM) + RMS_EPS) * gain

    scale_g = GQA_HEAD_DIM ** -0.5 * LOG2_E
    q_tiles = []
    for t in range(2):
        x = z[:, 768 + t * LANE: 768 + (t + 1) * LANE]
        x = _rope(head_norm(x, qg_ref[...]), cg, sg, GQA_HEAD_DIM // 4) * scale_g
        xr = pltpu.roll(x, GQA_HEAD_DIM, 1)
        if t == 0:
            q_tiles += [jnp.where(low, x, 0.0), jnp.where(low, xr, 0.0)]
        else:
            q_tiles += [jnp.where(low, 0.0, xr), jnp.where(low, 0.0, x)]
    qgo_ref[...] = jnp.concatenate(q_tiles, axis=1).astype(BF16)

    k = z[:, 1024:1152]
    kgo_ref[...] = _rope(head_norm(k, kg_ref[...]), cg, sg, GQA_HEAD_DIM // 4).astype(BF16)

    v = z[:, 1152:1280]
    ones_col = jnp.where(lane == GQA_HEAD_DIM, 1.0, 0.0)
    v0 = jnp.where(low, v, ones_col)
    v1 = jnp.where(low, pltpu.roll(v, GQA_HEAD_DIM, 1), ones_col)
    vgo_ref[...] = jnp.concatenate([jnp.transpose(v0), jnp.transpose(v1)], axis=0).astype(BF16)

    scale_m = (MLA_NOPE + MLA_ROPE) ** -0.5 * LOG2_E
    cq = _rms(z[:, 1280:1536], mqg_ref[...]).astype(BF16)
    qm = _dot(cq, wuq_ref[...])
    qm_tiles = [_rope(qm[:, t * LANE:(t + 1) * LANE], cm, sm, MLA_ROPE // 4) * scale_m
                for t in range(MLA_HEADS)]
    qmo_ref[...] = jnp.concatenate(qm_tiles, axis=1).astype(BF16)

    ckv = _rms(z[:, 1536:1664], mkg_ref[...]).astype(BF16)
    kr = _rope(z[:, 1664:1792], cm, sm, MLA_ROPE // 4)
    kn = _dot(ckv, wukn_ref[...])
    kmo_ref[...] = (kn + jnp.concatenate([kr] * MLA_HEADS, axis=1)).astype(BF16)
    vm = _dot(ckv, wuv_ref[...])
    vmo_ref[...] = jnp.concatenate([jnp.transpose(vm[:, t * LANE:(t + 1) * LANE] + ones_col)
                                    for t in range(MLA_HEADS)], axis=0).astype(BF16)


def _inproj_call(h, S, w_in_p, tabs, gsum, qg, kg, mqg, mkg, wuq, wukn, wuv, tm=512):
    T, D = h.shape
    nsb = S // tm
    cg, sg, cm, sm = tabs
    row = lambda w: pl.BlockSpec((tm, w), lambda i: (i, 0))
    tab = pl.BlockSpec((tm, LANE), lambda i: (i % nsb, 0))
    outs = [(256, F32, False), (256, F32, False), (512, BF16, False), (128, BF16, False), (256, BF16, True),
            (512, BF16, False), (512, BF16, False), (512, BF16, True)]
    col = lambda w: pl.BlockSpec((w, tm), lambda i: (0, i))
    return pl.pallas_call(
        _inproj_kernel,
        grid=(T // tm,),
        in_specs=[row(D), _full(w_in_p.shape), tab, tab, tab, tab, _full(gsum.shape), _full(qg.shape),
                  _full(kg.shape), _full(mqg.shape), _full(mkg.shape), _full(wuq.shape), _full(wukn.shape),
                  _full(wuv.shape)],
        out_specs=[col(w) if tr else row(w) for w, _, tr in outs],
        out_shape=[jax.ShapeDtypeStruct((w, T) if tr else (T, w), dt) for w, dt, tr in outs],
        compiler_params=_cparams(1),
        name="in_proj",
    )(h, w_in_p, cg, sg, cm, sm, gsum, qg, kg, mqg, mkg, wuq, wukn, wuv)


def _fft1_kernel(x_ref, chi_ref, clo_ref, o_ref):
    chi = chi_ref[...]
    clo = clo_ref[...]
    for j in range(8):
        xh, xl = _split(x_ref[0, :, j, :])
        o_ref[0, :, j, :] = _dot3(chi, clo, xh, xl)


def _fft2_kernel(re_ref, im_ref, mhi_ref, mlo_ref, chh_ref, chl_ref, wf_ref, bf_ref, o_ref):
    n2 = re_ref.shape[2]
    for j in range(8):
        ab = jnp.concatenate([re_ref[0, j], im_ref[0, j]], axis=0)
        abh, abl = _split(ab)
        ri = _dot3(mhi_ref[j], mlo_ref[j], abh, abl)
        rc = jnp.concatenate([ri[:n2], ri[n2:]], axis=1)
        rch, rcl = _split(rc)
        y = _dot3(rch, rcl, chh_ref[...], chl_ref[...])
        o_ref[0, :, j, :] = _dot(y.astype(BF16), wf_ref[...]) + bf_ref[...]


def _fnet_call(zf, B, S, consts, w_f, b_f):
    C = GROUP_W
    n1, n2 = DFT_N1, S // DFT_N1
    c1h, c1l, mh, ml, chh, chl = consts
    x4 = zf.reshape(B, n1, n2, C)
    ab = pl.pallas_call(
        _fft1_kernel,
        grid=(B, n2 // 8),
        in_specs=[pl.BlockSpec((1, n1, 8, C), lambda b, i: (b, 0, i, 0)), _full(c1h.shape), _full(c1l.shape)],
        out_specs=pl.BlockSpec((1, 2 * n1, 8, C), lambda b, i: (b, 0, i, 0)),
        out_shape=jax.ShapeDtypeStruct((B, 2 * n1, n2, C), F32),
        compiler_params=_cparams(2),
        name="fnet_stage1",
    )(x4, c1h, c1l)
    g = n1 // 8
    y = pl.pallas_call(
        _fft2_kernel,
        grid=(B, g),
        in_specs=[pl.BlockSpec((1, 8, n2, C), lambda b, i: (b, i, 0, 0)),
                  pl.BlockSpec((1, 8, n2, C), lambda b, i: (b, i + g, 0, 0)),
                  pl.BlockSpec((8, 2 * n2, 2 * n2), lambda b, i: (i, 0, 0)),
                  pl.BlockSpec((8, 2 * n2, 2 * n2), lambda b, i: (i, 0, 0)),
                  _full(chh.shape), _full(chl.shape), _full(w_f.shape), _full((1, C))],
        out_specs=pl.BlockSpec((1, n2, 8, C), lambda b, i: (b, 0, i, 0)),
        out_shape=jax.ShapeDtypeStruct((B, n2, n1, C), F32),
        compiler_params=_cparams(2),
        name="fnet_stage2",
    )(ab, ab, mh, ml, chh, chl, w_f, b_f.reshape(1, C))
    return y.reshape(B * S, C)


def _fnet_consts(S):
    n1, n2 = DFT_N1, S // DFT_N1
    k1 = np.arange(n1)
    ang1 = 2.0 * np.pi * np.outer(k1, k1) / n1
    c1 = np.concatenate([np.cos(ang1), np.sin(ang1)], axis=0)
    kk = k1[:, None, None] + n1 * np.arange(n2)[None, :, None]
    ang = 2.0 * np.pi * ((kk * np.arange(n2)[None, None, :]) % S) / S
    ct, st = np.cos(ang), np.sin(ang)
    m = np.concatenate([np.concatenate([ct, -st], axis=2),
                        np.concatenate([st, ct], axis=2)], axis=1)
    kc = np.arange(FNET_CH)
    angc = 2.0 * np.pi * np.outer(kc, kc) / FNET_CH
    eye = np.eye(GROUP_W // FNET_CH)
    ch = np.concatenate([np.kron(eye, np.cos(angc)), -np.kron(eye, np.sin(angc))], axis=0)

    def split(a):
        a32 = jnp.asarray(a, F32)
        hi = a32.astype(BF16)
        return hi, (a32 - hi.astype(F32)).astype(BF16)

    return split(c1) + split(m) + split(ch)


def _conv_kernel(prev_ref, cur_ref, next_ref, w_ref, b_ref, g_ref, beta_ref, wpw_ref, bpw_ref, o_ref, ext_ref):
    i = pl.program_id(1)
    ts = cur_ref.shape[1]
    H = CONV_HALO
    ext_ref[0:H, :] = jnp.where(i > 0, prev_ref[0], 0.0)
    ext_ref[H:H + ts, :] = cur_ref[0]
    ext_ref[H + ts:H + ts + H, :] = jnp.where(i < pl.num_programs(1) - 1, next_ref[0], 0.0)
    acc = jnp.zeros((ts, cur_ref.shape[2]), F32)
    for k in range(CONV_K):
        off = H - CONV_K // 2 + k
        acc = acc + ext_ref[off:off + ts, :] * w_ref[k:k + 1, :]
    y = _layer_norm(acc + b_ref[...], g_ref[...], beta_ref[...])
    y = y * (1.0 / (1.0 + jnp.exp(-y)))
    o_ref[...] = _dot(y.astype(BF16), wpw_ref[...]) + bpw_ref[...]


def _conv_call(u, B, S, dw_w, dw_b, cg, cb, w_pw, b_pw, ts=512):
    C = GROUP_W
    H = CONV_HALO
    nb = S // ts
    r = ts // H
    u3 = u.reshape(B, S, C)
    vec = lambda a: a.reshape(1, C)
    return pl.pallas_call(
        _conv_kernel,
        grid=(B, nb),
        in_specs=[pl.BlockSpec((1, H, C), lambda b, i: (b, jnp.maximum(i * r - 1, 0), 0)),
                  pl.BlockSpec((1, ts, C), lambda b, i: (b, i, 0)),
                  pl.BlockSpec((1, H, C), lambda b, i: (b, jnp.minimum((i + 1) * r, S // H - 1), 0)),
                  _full((CONV_K, C)), _full((1, C)), _full((1, C)), _full((1, C)), _full((C, C)), _full((1, C))],
        out_specs=pl.BlockSpec((ts, C), lambda b, i: (b * nb + i, 0)),
        out_shape=jax.ShapeDtypeStruct((B * S, C), F32),
        scratch_shapes=[pltpu.VMEM((ts + 2 * H, C), F32)],
        compiler_params=_cparams(2),
        name="conv_module",
    )(u3, u3, u3, dw_w, vec(dw_b), vec(cg), vec(cb), w_pw, vec(b_pw))


def _attn_kernel(q_ref, k_ref, vt_ref, o_ref, s0_ref, s1_ref, *, k_tile, v_tile, tk):
    tq = q_ref.shape[1]
    S = k_ref.shape[1]
    n = S // tk
    dv = GQA_HEAD_DIM
    s_bufs = (s0_ref, s1_ref)
    outs = []
    for h in range(len(k_tile)):
        q = q_ref[0, :, h * LANE:(h + 1) * LANE]
        kb, vb = k_tile[h], v_tile[h]

        def qk(c, s_ref):
            k = k_ref[0, c * tk:(c + 1) * tk, kb * LANE:(kb + 1) * LANE]
            s_ref[...] = lax.dot_general(k, q, (((1,), (1,)), ((), ())), preferred_element_type=F32)

        def softmax_pv(c, s_ref, m, acc):
            vt = vt_ref[vb * LANE:(vb + 1) * LANE, c * tk:(c + 1) * tk]
            st = s_ref[...]
            m_new = jnp.maximum(m, jnp.max(st, axis=0, keepdims=True))
            pt = jnp.exp2(st - m_new).astype(BF16)
            acc = jnp.exp2(m - m_new) * acc + _dot(vt, pt)
            return m_new, acc

        m = jnp.full((1, tq), NEG_BIG, F32)
        acc = jnp.zeros((LANE, tq), F32)
        qk(0, s_bufs[0])
        for c in range(n):
            if c + 1 < n:
                qk(c + 1, s_bufs[(c + 1) % 2])
            m, acc = softmax_pv(c, s_bufs[c % 2], m, acc)
        outs.append(acc[:dv] / acc[dv:dv + 1])
    o_ref[0] = jnp.transpose(jnp.concatenate(outs, axis=0))


def _attn_call(q, k, vt, B, S, k_tile, v_tile, name, tq=1024, tk=2048):
    nh = len(k_tile)
    tq, tk = min(tq, S), min(tk, S)
    q3 = q.reshape(B, S, q.shape[-1])
    k3 = k.reshape(B, S, k.shape[-1])
    kern = functools.partial(_attn_kernel, k_tile=k_tile, v_tile=v_tile, tk=tk)
    out = pl.pallas_call(
        kern,
        grid=(B, S // tq),
        in_specs=[pl.BlockSpec((1, tq, q3.shape[-1]), lambda b, i: (b, i, 0)),
                  pl.BlockSpec((1, S, k3.shape[-1]), lambda b, i: (b, 0, 0), pipeline_mode=pl.Buffered(1)),
                  pl.BlockSpec((vt.shape[0], S), lambda b, i: (0, b), pipeline_mode=pl.Buffered(1))],
        out_specs=pl.BlockSpec((1, tq, nh * GQA_HEAD_DIM), lambda b, i: (b, i, 0)),
        out_shape=jax.ShapeDtypeStruct((B, S, nh * GQA_HEAD_DIM), F32),
        scratch_shapes=[pltpu.VMEM((tk, tq), F32), pltpu.VMEM((tk, tq), F32)],
        compiler_params=_cparams(2),
        name=name,
    )(q3, k3, vt)
    return out.reshape(B * S, nh * GQA_HEAD_DIM)


def _mixout_kernel(yf_ref, yc_ref, yg_ref, ym_ref, h_ref, gg_ref, wo_ref, g_ref, b_ref, o_ref):
    parts = [_rms(r[...], gg_ref[i:i + 1, :]).astype(BF16)
             for i, r in enumerate((yf_ref, yc_ref, yg_ref, ym_ref))]
    mix = _dot(jnp.concatenate(parts, axis=1), wo_ref[...])
    o_ref[...] = _layer_norm(DN_ALPHA * h_ref[...] + mix, g_ref[...], b_ref[...])


def _mixout_call(ys, h, grp_g, w_o, g, b, tm=512):
    T, D = h.shape
    C = GROUP_W
    row = lambda w: pl.BlockSpec((tm, w), lambda i: (i, 0))
    return pl.pallas_call(
        _mixout_kernel,
        grid=(T // tm,),
        in_specs=[row(C)] * 4 + [row(D), _full((N_GROUPS, C)), _full((D, D)), _full((1, D)), _full((1, D))],
        out_specs=row(D),
        out_shape=jax.ShapeDtypeStruct((T, D), F32),
        compiler_params=_cparams(1),
        name="mix_out",
    )(*ys, h, grp_g, w_o, g.reshape(1, D), b.reshape(1, D))


def _memkv_kernel(m_ref, wk_ref, wv_ref, k_ref, v_ref):
    m = m_ref[...].astype(BF16)
    k_ref[...] = _dot(m, wk_ref[...]).astype(BF16)
    v_ref[...] = _dot(m, wv_ref[...]).astype(BF16)


def _memkv_call(mem2, w_xk, w_xv, tm=256):
    R, D = mem2.shape
    row = pl.BlockSpec((tm, D), lambda i: (i, 0))
    return pl.pallas_call(
        _memkv_kernel,
        grid=(R // tm,),
        in_specs=[row, _full((D, D)), _full((D, D))],
        out_specs=[row, row],
        out_shape=[jax.ShapeDtypeStruct((R, D), BF16)] * 2,
        compiler_params=_cparams(1),
        name="mem_kv",
    )(mem2, w_xk, w_xv)


def _xattn_kernel(h_ref, wq_ref, k_ref, v_ref, wo_ref, g_ref, b_ref, o_ref, o3_ref):
    h = h_ref[0]
    q = (_dot(h.astype(BF16), wq_ref[...]) * (X_HEAD_DIM ** -0.5)).astype(BF16)
    outs = []
    for hd in range(X_HEADS):
        sl = slice(hd * X_HEAD_DIM, (hd + 1) * X_HEAD_DIM)
        s = lax.dot_general(q[:, sl], k_ref[0, :, sl], (((1,), (1,)), ((), ())), preferred_element_type=F32)
        e = jnp.exp(s - jnp.max(s, axis=-1, keepdims=True))
        o = _dot(e.astype(BF16), v_ref[0, :, sl]) / jnp.sum(e, axis=-1, keepdims=True)
        outs.append(o.astype(BF16))
    xa = _dot(jnp.concatenate(outs, axis=1), wo_ref[...])
    hn = _layer_norm(DN_ALPHA * h + xa, g_ref[...], b_ref[...])
    o_ref[0] = hn
    tm = hn.shape[0]
    ns = D_MODEL // LANE
    for s in range(ns):
        o3_ref[pl.ds(s, tm, stride=ns), :] = hn[:, s * LANE:(s + 1) * LANE]


def _xattn_call(h, B, S, w_xq, kmem, vmem, w_xo, g, b, tm=256):
    D = D_MODEL
    M = kmem.shape[0] // B
    ns = D // LANE
    nsb = S // tm
    row = pl.BlockSpec((1, tm, D), lambda bb, i: (bb, i, 0))
    kv = pl.BlockSpec((1, M, D), lambda bb, i: (bb, 0, 0))
    out, out3 = pl.pallas_call(
        _xattn_kernel,
        grid=(B, nsb),
        in_specs=[row, _full((D, D)), kv, kv, _full((D, D)), _full((1, D)), _full((1, D))],
        out_specs=[row, pl.BlockSpec((tm * ns, LANE), lambda bb, i: (bb * nsb + i, 0))],
        out_shape=[jax.ShapeDtypeStruct((B, S, D), F32), jax.ShapeDtypeStruct((B * S * ns, LANE), F32)],
        compiler_params=_cparams(2),
        name="cross_attn",
    )(h.reshape(B, S, D), w_xq, kmem.reshape(B, M, D), vmem.reshape(B, M, D), w_xo, g.reshape(1, D), b.reshape(1, D))
    return out.reshape(B * S, D), out3.reshape(B * S, ns, LANE)


def _router_kernel(h_ref, whi_ref, wlo_ref, b_ref, tri_ref, idx_ref, rank_ref, gate_ref, cnt_ref, carry_ref):
    @pl.when(pl.program_id(0) == 0)
    def _():
        carry_ref[...] = jnp.zeros(carry_ref.shape, F32)

    hh, hl = _split(h_ref[...])
    logits = _dot3(hh, hl, whi_ref[...], wlo_ref[...]) + b_ref[...]
    lane = lax.broadcasted_iota(jnp.int32, logits.shape, 1)
    idx_out = jnp.zeros(logits.shape, F32)
    val_out = jnp.zeros(logits.shape, F32)
    onehot = jnp.zeros(logits.shape, F32)
    picks = []
    top = None
    denom = jnp.zeros((logits.shape[0], 1), F32)
    for k in range(TOP_K):
        mx = jnp.max(logits, axis=-1, keepdims=True)
        pick = jnp.min(jnp.where(logits == mx, lane, LANE), axis=-1, keepdims=True)
        if k == 0:
            top = mx
        e = jnp.exp(mx - top)
        denom = denom + e
        hit = lane == pick
        idx_out = jnp.where(lane == k, pick.astype(F32), idx_out)
        val_out = jnp.where(lane == k, e, val_out)
        onehot = jnp.where(hit, 1.0, onehot)
        logits = jnp.where(hit, NEG_BIG * 2.0, logits)
        picks.append(pick)
    gate_ref[...] = val_out / denom

    before = _dot(tri_ref[...], onehot.astype(BF16)) + carry_ref[0:1, :]
    rank_out = jnp.zeros(logits.shape, F32)
    for k, pick in enumerate(picks):
        r = jnp.sum(jnp.where(lane == pick, before, 0.0), axis=-1, keepdims=True)
        rank_out = jnp.where(lane == k, r, rank_out)
    total = carry_ref[0:1, :] + jnp.sum(onehot, axis=0, keepdims=True)
    carry_ref[...] = jnp.broadcast_to(total, carry_ref.shape)
    cnt_ref[...] = jnp.broadcast_to(total, cnt_ref.shape)
    idx_ref[...] = jnp.transpose(idx_out)[:8, :].astype(jnp.int32)
    rank_ref[...] = jnp.transpose(rank_out)[:8, :].astype(jnp.int32)


def _router_call(h, w_hi, w_lo, b_pad, tm=1024):
    T, D = h.shape
    tri = jnp.asarray(np.tril(np.ones((tm, tm), np.float32), -1), BF16)
    row = lambda w: pl.BlockSpec((tm, w), lambda i: (i, 0))
    col = pl.BlockSpec((8, tm), lambda i: (0, i))
    return pl.pallas_call(
        _router_kernel,
        grid=(T // tm,),
        in_specs=[row(D), _full((D, LANE)), _full((D, LANE)), _full((1, LANE)), _full((tm, tm))],
        out_specs=[col, col, row(LANE), _full((8, LANE))],
        out_shape=[jax.ShapeDtypeStruct((8, T), jnp.int32), jax.ShapeDtypeStruct((8, T), jnp.int32),
                   jax.ShapeDtypeStruct((T, LANE), F32), jax.ShapeDtypeStruct((8, LANE), F32)],
        scratch_shapes=[pltpu.VMEM((8, LANE), F32)],
        compiler_params=_cparams(1),
        name="router",
    )(h, w_hi, w_lo, b_pad, tri)


def _moe_layout(counts, idx_t, rank_t, T):
    nb = T * TOP_K // MOE_BLOCK + N_EXPERTS
    cnt = counts[0, :N_EXPERTS].astype(jnp.int32)
    nblk = (cnt + MOE_BLOCK - 1) // MOE_BLOCK
    bend = jnp.cumsum(nblk)
    bstart = bend - nblk
    blk = jnp.arange(nb, dtype=jnp.int32)
    block_e = jnp.minimum(jnp.sum((blk[:, None] >= bend[None, :]).astype(jnp.int32), axis=1), N_EXPERTS - 1)
    first_row = bstart * MOE_BLOCK
    e4 = idx_t[:TOP_K]
    base = jnp.zeros_like(e4)
    for e in range(N_EXPERTS):
        base = jnp.where(e4 == e, first_row[e], base)
    pos = (base + rank_t[:TOP_K]).astype(jnp.int32)
    meta = jnp.concatenate([jnp.where(nblk > 0, bend - 1, -1), bend[-1:]]).astype(jnp.int32)
    return block_e.astype(jnp.int32), pos, meta


DISPATCH_TOKENS = 1024
DISPATCH_UNROLL = 8


def _dispatch_kernel(meta_ref, pos_ref, h_ref, x_hbm, zbuf, sem):
    i = pl.program_id(0)
    td = h_ref.shape[0]
    nb = x_hbm.shape[0] // MOE_BLOCK

    def zero_block(b):
        return pltpu.make_async_copy(zbuf, x_hbm.at[pl.ds(b * MOE_BLOCK, MOE_BLOCK)], sem)

    @pl.when(i == 0)
    def _():
        zbuf[...] = jnp.zeros(zbuf.shape, F32)
        used = meta_ref[N_EXPERTS]
        for phase in ("start", "wait"):
            for e in range(N_EXPERTS):
                @pl.when(meta_ref[e] >= 0)
                def _():
                    getattr(zero_block(meta_ref[e]), phase)()

            def tail(b, _):
                getattr(zero_block(b), phase)()
                return 0

            lax.fori_loop(used, nb, tail, 0)

    def body(g, _):
        t0 = g * DISPATCH_UNROLL
        rows = [[pos_ref[0, 0, k * td + t0 + u] for k in range(TOP_K)] for u in range(DISPATCH_UNROLL)]
        for u in range(DISPATCH_UNROLL):
            for k in range(TOP_K):
                pltpu.make_async_copy(h_ref.at[t0 + u], x_hbm.at[rows[u][k]], sem).start(priority=k % 2)
        return 0

    lax.fori_loop(0, td // DISPATCH_UNROLL, body, 0)
    for _ in range(TOP_K):
        pltpu.make_async_copy(h_ref, x_hbm.at[pl.ds(0, td)], sem).wait()


def _dispatch_call(h3, pos, meta, nb):
    T = h3.shape[0]
    td = DISPATCH_TOKENS
    grid_spec = pltpu.PrefetchScalarGridSpec(
        num_scalar_prefetch=1,
        grid=(T // td,),
        in_specs=[pl.BlockSpec((1, 1, TOP_K * td), lambda i, m: (i, 0, 0), memory_space=pltpu.SMEM),
                  pl.BlockSpec((td,) + h3.shape[1:], lambda i, m: (i, 0, 0))],
        out_specs=pl.BlockSpec(memory_space=pl.ANY),
        scratch_shapes=[pltpu.VMEM((MOE_BLOCK,) + h3.shape[1:], F32), pltpu.SemaphoreType.DMA(())],
    )
    return pl.pallas_call(
        _dispatch_kernel,
        grid_spec=grid_spec,
        out_shape=jax.ShapeDtypeStruct((nb * MOE_BLOCK,) + h3.shape[1:], F32),
        compiler_params=_cparams(1),
        name="dispatch",
    )(meta, _tile_major(pos, td), h3)


MOE_CHUNKS = 4


def _moe_kernel(be_ref, meta_ref, x_ref, wgu_ref, bgu_ref, wdn_ref, bdn_ref, o_ref, wgu_bf, wdn_bf, act_ref):
    j = pl.program_id(0)
    ns = D_MODEL // LANE

    @pl.when((j == 0) | (be_ref[j] != be_ref[jnp.maximum(j - 1, 0)]))
    def _():
        wgu_bf[...] = wgu_ref[0, 0].astype(BF16)
        wdn_bf[...] = wdn_ref[0, 0].astype(BF16)

    @pl.when(j < meta_ref[N_EXPERTS])
    def _():
        x = jnp.concatenate([x_ref[pl.ds(s, MOE_BLOCK, stride=ns), :] for s in range(ns)], axis=1).astype(BF16)
        cw = D_FF // MOE_CHUNKS
        for c in range(MOE_CHUNKS):
            gate = _dot(x, wgu_bf[:, c * cw:(c + 1) * cw]) + bgu_ref[0, 0, :, c * cw:(c + 1) * cw]
            up = _dot(x, wgu_bf[:, D_FF + c * cw:D_FF + (c + 1) * cw]) + bgu_ref[0, 0, :, D_FF + c * cw:D_FF + (c + 1) * cw]
            gate = jnp.minimum(gate, SWIGLU_LIMIT)
            up = jnp.clip(up, -SWIGLU_LIMIT, SWIGLU_LIMIT)
            act = (up + 1.0) * (gate * (1.0 / (1.0 + jnp.exp(-SWIGLU_ALPHA * gate))))
            act_ref[:, c * cw:(c + 1) * cw] = act.astype(BF16)
        ow = D_MODEL // MOE_CHUNKS
        for n in range(MOE_CHUNKS):
            yn = _dot(act_ref[...], wdn_bf[:, n * ow:(n + 1) * ow]) + bdn_ref[0, 0, :, n * ow:(n + 1) * ow]
            for t in range(ow // LANE):
                s = n * (ow // LANE) + t
                o_ref[pl.ds(s, MOE_BLOCK, stride=ns), :] = yn[:, t * LANE:(t + 1) * LANE]

    @pl.when(j >= meta_ref[N_EXPERTS])
    def _():
        o_ref[...] = jnp.zeros(o_ref.shape, F32)


def _moe_call(xs3, block_e, meta, layer, w_gu, b_gu, w_down, b_down):
    D = D_MODEL
    ns = D // LANE
    nb = block_e.shape[0]
    L, E = w_gu.shape[:2]
    rows = MOE_BLOCK * ns
    by_e = lambda shp: pl.BlockSpec((1, 1) + shp, lambda j, be, m: (layer, be[j], 0, 0))
    grid_spec = pltpu.PrefetchScalarGridSpec(
        num_scalar_prefetch=2,
        grid=(nb,),
        in_specs=[pl.BlockSpec((rows, LANE), lambda j, be, m: (jnp.minimum(j, m[N_EXPERTS] - 1), 0)),
                  by_e((D, 2 * D_FF)), by_e((1, 2 * D_FF)), by_e((D_FF, D)), by_e((1, D))],
        out_specs=pl.BlockSpec((rows, LANE), lambda j, be, m: (j, 0)),
        scratch_shapes=[pltpu.VMEM((D, 2 * D_FF), BF16), pltpu.VMEM((D_FF, D), BF16),
                        pltpu.VMEM((MOE_BLOCK, D_FF), BF16)],
    )
    ys = pl.pallas_call(
        _moe_kernel,
        grid_spec=grid_spec,
        out_shape=jax.ShapeDtypeStruct((nb * rows, LANE), F32),
        compiler_params=_cparams(1),
        name="routed_experts",
    )(block_e, meta, xs3.reshape(nb * rows, LANE), w_gu, b_gu.reshape(L, E, 1, 2 * D_FF), w_down,
      b_down.reshape(L, E, 1, D))
    return ys.reshape(xs3.shape)


COMBINE_UNROLL = 8


def _tile_major(pos, tile):
    T = pos.shape[1]
    return pos.reshape(TOP_K, T // tile, tile).transpose(1, 0, 2).reshape(T // tile, 1, TOP_K * tile)


def _combine_kernel(posn_ref, pos0_ref, gate_ref, h_ref, y_hbm, g_ref, b_ref, o_ref, ybuf, sems):
    i = pl.program_id(0)
    tm = h_ref.shape[0]
    ns = D_MODEL // LANE
    slot = i % 2

    def copy(row, sl, k, r):
        return pltpu.make_async_copy(y_hbm.at[row], ybuf.at[sl, k, :, r], sems.at[sl])

    def issue(pos_ref, sl):
        def body(g, _):
            r0 = g * COMBINE_UNROLL
            rows = [[pos_ref[0, 0, k * tm + r0 + u] for k in range(TOP_K)] for u in range(COMBINE_UNROLL)]
            for u in range(COMBINE_UNROLL):
                for k in range(TOP_K):
                    copy(rows[u][k], sl, k, r0 + u).start(priority=k % 2)
            return 0

        lax.fori_loop(0, tm // COMBINE_UNROLL, body, 0)

    @pl.when(i == 0)
    def _():
        issue(pos0_ref, 0)

    @pl.when(i + 1 < pl.num_programs(0))
    def _():
        issue(posn_ref, 1 - slot)

    pltpu.make_async_copy(ybuf.at[slot], ybuf.at[slot], sems.at[slot]).wait()
    gates = gate_ref[...]
    ff = jnp.zeros(h_ref.shape, F32)
    for k in range(TOP_K):
        y = jnp.concatenate([ybuf[slot, k, s] for s in range(ns)], axis=1)
        ff = ff + gates[:, k:k + 1] * y
    o_ref[...] = _layer_norm(DN_ALPHA * h_ref[...] + ff, g_ref[...], b_ref[...])


def _combine_call(ys3, pos, gates, h, g, b, tm=256):
    T, D = h.shape
    nt = T // tm
    ns = D // LANE
    row = lambda w: pl.BlockSpec((tm, w), lambda i: (i, 0))
    pos = _tile_major(pos, tm)
    smem = lambda f: pl.BlockSpec((1, 1, TOP_K * tm), lambda i: (f(i), 0, 0), memory_space=pltpu.SMEM)
    return pl.pallas_call(
        _combine_kernel,
        grid=(nt,),
        in_specs=[smem(lambda i: jnp.minimum(i + 1, nt - 1)), smem(lambda i: 0), row(LANE), row(D),
                  pl.BlockSpec(memory_space=pl.ANY), _full((1, D)), _full((1, D))],
        out_specs=row(D),
        out_shape=jax.ShapeDtypeStruct((T, D), F32),
        scratch_shapes=[pltpu.VMEM((2, TOP_K, ns, tm, LANE), F32), pltpu.SemaphoreType.DMA((2,))],
        compiler_params=_cparams(1),
        name="combine",
    )(pos, pos, gates, h, ys3, g.reshape(1, D), b.reshape(1, D))


def _rope_tables(S):
    pos = np.arange(S)
    row, col = pos // GRID_W, pos % GRID_W

    def tab(p, dim):
        inv = ROPE_THETA ** (-np.arange(0, dim, 2, dtype=np.float64) / dim)
        ang = p[:, None].astype(np.float64) * inv[None, :]
        return np.cos(ang), np.sin(ang)

    def pair(dim):
        cr, sr = tab(row, dim)
        cc, sc = tab(col, dim)
        return (np.concatenate([cr, cr, cc, cc], axis=1), np.concatenate([-sr, sr, -sc, sc], axis=1))

    cg, sg = pair(GQA_HEAD_DIM // 2)
    cg, sg = np.tile(cg, (1, 2)), np.tile(sg, (1, 2))
    cm32, sm32 = pair(MLA_ROPE // 2)
    cm = np.ones((S, LANE))
    sm = np.zeros((S, LANE))
    cm[:, MLA_NOPE:MLA_NOPE + MLA_ROPE] = cm32
    sm[:, MLA_NOPE:MLA_NOPE + MLA_ROPE] = sm32
    return tuple(jnp.asarray(a, F32) for a in (cg, sg, cm, sm))


def _pad_heads(w, n_heads, width, lo, hi):
    r = w.shape[0]
    w3 = w.reshape(r, n_heads, width)[:, :, lo:hi]
    w3 = jnp.pad(w3, ((0, 0), (0, 0), (0, HEAD_PAD - (hi - lo))))
    return w3.reshape(r, n_heads * HEAD_PAD).astype(BF16)


def kernel(x, mem, ln_in_g, ln_in_b, w_in, w_f, b_f, dw_w, dw_b, conv_ln_g, conv_ln_b, w_pw, b_pw, q_norm_g, k_norm_g, mla_q_norm_g, w_uq, mla_kv_norm_g, w_ukv, grp_norm_g, w_o, ln1_g, ln1_b, w_xq, w_xk, w_xv, w_xo, ln2_g, ln2_b, w_router, b_router, w_gu, b_gu, w_down, b_down, ln3_g, ln3_b):
    B, S, D = x.shape
    T = B * S
    depth = w_in.shape[0]
    tabs = _rope_tables(S)
    fconsts = _fnet_consts(S)
    lane = np.arange(LANE)
    gsum = jnp.asarray((lane[:, None] // GQA_HEAD_DIM) == (lane[None, :] // GQA_HEAD_DIM), BF16)
    mem2 = mem.reshape(B * mem.shape[1], D)
    kr_lo = w_in.shape[2] - MLA_ROPE

    h = _ln_call(x.reshape(T, D), ln_in_g, ln_in_b)
    for l in range(depth):
        zeros = lambda n: jnp.zeros((D, n), F32)
        w_in_p = jnp.concatenate([w_in[l][:, :kr_lo], zeros(MLA_NOPE), w_in[l][:, kr_lo:],
                                  zeros(LANE - MLA_NOPE - MLA_ROPE)], axis=1).astype(BF16)
        wuq = _pad_heads(w_uq[l], MLA_HEADS, MLA_NOPE + MLA_ROPE, 0, MLA_NOPE + MLA_ROPE)
        wukn = _pad_heads(w_ukv[l], MLA_HEADS, MLA_NOPE + MLA_V, 0, MLA_NOPE)
        wuv = _pad_heads(w_ukv[l], MLA_HEADS, MLA_NOPE + MLA_V, MLA_NOPE, MLA_NOPE + MLA_V)
        qg = jnp.tile(q_norm_g[l], 2).reshape(1, LANE)
        kg = jnp.tile(k_norm_g[l], 2).reshape(1, LANE)
        zf, u, qgq, kgk, vgv, qm, km, vm = _inproj_call(
            h, S, w_in_p, tabs, gsum, qg, kg, mla_q_norm_g[l].reshape(1, -1), mla_kv_norm_g[l].reshape(1, -1),
            wuq, wukn, wuv)

        yf = _fnet_call(zf, B, S, fconsts, w_f[l].astype(BF16), b_f[l])
        yc = _conv_call(u, B, S, dw_w[l], dw_b[l], conv_ln_g[l], conv_ln_b[l], w_pw[l].astype(BF16), b_pw[l])
        yg = _attn_call(qgq, kgk, vgv, B, S, (0, 0, 0, 0), (0, 0, 1, 1), "gqa_attn")
        ym = _attn_call(qm, km, vm, B, S, (0, 1, 2, 3), (0, 1, 2, 3), "mla_attn")
        h = _mixout_call((yf, yc, yg, ym), h, grp_norm_g[l], w_o[l].astype(BF16), ln1_g[l], ln1_b[l])

        kmem, vmem = _memkv_call(mem2, w_xk[l].astype(BF16), w_xv[l].astype(BF16))
        h, h3 = _xattn_call(h, B, S, w_xq[l].astype(BF16), kmem, vmem, w_xo[l].astype(BF16), ln2_g[l], ln2_b[l])

        w_r = jnp.pad(w_router[l], ((0, 0), (0, LANE - N_EXPERTS)))
        w_r_hi = w_r.astype(BF16)
        w_r_lo = (w_r - w_r_hi.astype(F32)).astype(BF16)
        b_r = jnp.concatenate([b_router[l], jnp.full((LANE - N_EXPERTS,), NEG_BIG, F32)]).reshape(1, LANE)
        idx_t, rank_t, gates, counts = _router_call(h, w_r_hi, w_r_lo, b_r)
        block_e, pos, meta = _moe_layout(counts, idx_t, rank_t, T)
        xs3 = _dispatch_call(h3, pos, meta, block_e.shape[0])
        ys3 = _moe_call(xs3, block_e, meta, l, w_gu, b_gu, w_down, b_down)
        h = _combine_call(ys3, pos, gates, h, ln3_g[l], ln3_b[l])
    return h.reshape(B, S, D)
```
